```python
import math
import jax, jax.numpy as jnp
from jax import lax
import numpy as np

D_MODEL = 1024
BATCH = 16
SEQ = 2048
DEPTH = 4

HEAD_DIM = 64
N_HEADS_SB = D_MODEL // (2 * HEAD_DIM)
N_HEADS_FOX = D_MODEL // (2 * HEAD_DIM)
N_HEADS_DIFF = D_MODEL // (2 * HEAD_DIM)
DIFF_V_DIM = 2 * HEAD_DIM
D_FF = ((8 * D_MODEL + 3 * 256 - 1) // (3 * 256)) * 256
N_BUCKETS = 32
MAX_DISTANCE = 128
Q_BLOCK = 128
RMS_EPS = 1e-6

SB_WIDTH = N_HEADS_SB * HEAD_DIM
FOX_WIDTH = N_HEADS_FOX * HEAD_DIM
EVEN_IN = 3 * SB_WIDTH + 3 * FOX_WIDTH + N_HEADS_FOX
EVEN_MIX = SB_WIDTH + FOX_WIDTH
DIFF_QK = N_HEADS_DIFF * 2 * HEAD_DIM
DIFF_V = N_HEADS_DIFF * DIFF_V_DIM
DIFF_IN = 2 * DIFF_QK + DIFF_V
N_EVEN = (DEPTH + 1) // 2
N_ODD = DEPTH // 2

kernel_name = "hybrid_stickbreak_fox_diffattn_trunk"


def rms_norm(x, g):
    xf = x.astype(jnp.float32)
    y = xf * lax.rsqrt(jnp.mean(xf * xf, axis=-1, keepdims=True) + RMS_EPS)
    return y * g.astype(jnp.float32)


def split_heads(t, n_heads, dim):
    b, s, _ = t.shape
    return t.reshape(b, s, n_heads, dim).transpose(0, 2, 1, 3)


def merge_heads(t):
    b, h, s, d = t.shape
    return t.transpose(0, 2, 1, 3).reshape(b, s, h * d)


def sweep_query_blocks(block_fn, seq_len):
    return jnp.concatenate(
        [block_fn(i * Q_BLOCK, (i + 1) * Q_BLOCK) for i in range(seq_len // Q_BLOCK)], axis=2)


def stick_breaking_attention(q, k, v):
    scale = HEAD_DIM ** -0.5

    def block(start, end):
        z = jnp.einsum('bhqd,bhkd->bhqk', q[:, :, start:end], k[:, :, :end]) * scale
        t_pos = jnp.arange(start, end)[:, None]
        s_pos = jnp.arange(end)[None, :]
        strict = s_pos < t_pos
        log_1m_beta = jnp.where(strict, jax.nn.log_sigmoid(-z), 0.0)
        between = lax.cumsum(log_1m_beta, axis=3, reverse=True) - log_1m_beta
        w = jnp.where(strict, jnp.exp(jax.nn.log_sigmoid(z) + between), 0.0)
        return jnp.einsum('bhqk,bhkd->bhqd', w, v[:, :, :end])

    return sweep_query_blocks(block, q.shape[2])


def forgetting_attention(q, k, v, log_f):
    scale = HEAD_DIM ** -0.5
    c = jnp.cumsum(log_f, axis=-1)

    def block(start, end):
        z = jnp.einsum('bhqd,bhkd->bhqk', q[:, :, start:end], k[:, :, :end]) * scale
        z = z + c[:, :, start:end, None] - c[:, :, None, :end]
        causal = jnp.arange(end)[None, :] <= jnp.arange(start, end)[:, None]
        p = jax.nn.softmax(jnp.where(causal, z, -jnp.inf), axis=-1)
        return jnp.einsum('bhqk,bhkd->bhqd', p, v[:, :, :end])

    return sweep_query_blocks(block, q.shape[2])


def t5_bucket(dist):
    max_exact = N_BUCKETS // 2
    nf = jnp.maximum(dist, 1).astype(jnp.float32)
    large = max_exact + (jnp.log(nf / max_exact) / math.log(MAX_DISTANCE / max_exact)
                         * (N_BUCKETS - max_exact)).astype(jnp.int32)
    large = jnp.minimum(large, N_BUCKETS - 1)
    return jnp.where(dist < max_exact, dist, large)


def differential_attention(q1, q2, k1, k2, v, lam, rel_bias):
    scale = HEAD_DIM ** -0.5
    table = rel_bias.astype(jnp.float32)

    def block(start, end):
        t_pos = jnp.arange(start, end)[:, None]
        s_pos = jnp.arange(end)[None, :]
        causal = s_pos <= t_pos
        bias = table[t5_bucket(jnp.maximum(t_pos - s_pos, 0))].transpose(2, 0, 1)[None]
        s1 = jnp.einsum('bhqd,bhkd->bhqk', q1[:, :, start:end], k1[:, :, :end]) * scale + bias
        s2 = jnp.einsum('bhqd,bhkd->bhqk', q2[:, :, start:end], k2[:, :, :end]) * scale + bias
        p1 = jax.nn.softmax(jnp.where(causal, s1, -jnp.inf), axis=-1)
        p2 = jax.nn.softmax(jnp.where(causal, s2, -jnp.inf), axis=-1)
        return jnp.einsum('bhqk,bhkd->bhqd', p1 - lam * p2, v[:, :, :end])

    return sweep_query_blocks(block, q1.shape[2])


def even_mixer(h, w_in, forget_b, gq, gk, w_out):
    proj = jnp.einsum('bsd,de->bse', h, w_in)
    cuts = [SB_WIDTH, 2 * SB_WIDTH, 3 * SB_WIDTH, 3 * SB_WIDTH + FOX_WIDTH,
            3 * SB_WIDTH + 2 * FOX_WIDTH, 3 * SB_WIDTH + 3 * FOX_WIDTH]
    qa, ka, va, qb, kb, vb, fg = jnp.split(proj, cuts, axis=-1)
    f32 = jnp.float32
    o_a = stick_breaking_attention(split_heads(qa, N_HEADS_SB, HEAD_DIM).astype(f32),
                                   split_heads(ka, N_HEADS_SB, HEAD_DIM).astype(f32),
                                   split_heads(va, N_HEADS_SB, HEAD_DIM).astype(f32))
    qb = rms_norm(split_heads(qb, N_HEADS_FOX, HEAD_DIM), gq)
    kb = rms_norm(split_heads(kb, N_HEADS_FOX, HEAD_DIM), gk)
    log_f = jax.nn.log_sigmoid(fg.astype(f32) + forget_b.astype(f32)).transpose(0, 2, 1)
    o_b = forgetting_attention(qb, kb, split_heads(vb, N_HEADS_FOX, HEAD_DIM).astype(f32), log_f)
    o = jnp.concatenate([merge_heads(o_a), merge_heads(o_b)], axis=-1).astype(h.dtype)
    return jnp.einsum('bse,ed->bsd', o, w_out)


def diff_mixer(h, w_in, gq, gk, lq1, lk1, lq2, lk2, subln_g, w_out, rel_bias, layer_idx):
    b, s, _ = h.shape
    proj = jnp.einsum('bsd,de->bse', h, w_in)
    q, k, v = jnp.split(proj, [DIFF_QK, 2 * DIFF_QK], axis=-1)
    q = rms_norm(q.reshape(b, s, N_HEADS_DIFF, 2, HEAD_DIM), gq)
    k = rms_norm(k.reshape(b, s, N_HEADS_DIFF, 2, HEAD_DIM), gk)
    q1, q2 = q[..., 0, :].transpose(0, 2, 1, 3), q[..., 1, :].transpose(0, 2, 1, 3)
    k1, k2 = k[..., 0, :].transpose(0, 2, 1, 3), k[..., 1, :].transpose(0, 2, 1, 3)
    v = split_heads(v, N_HEADS_DIFF, DIFF_V_DIM).astype(jnp.float32)
    lam_init = 0.8 - 0.6 * math.exp(-0.3 * layer_idx)
    f32 = jnp.float32
    lam = (jnp.exp(jnp.sum(lq1.astype(f32) * lk1.astype(f32)))
           - jnp.exp(jnp.sum(lq2.astype(f32) * lk2.astype(f32))) + lam_init)
    o = differential_attention(q1, q2, k1, k2, v, lam, rel_bias)
    o = rms_norm(o, subln_g) * (1.0 - lam_init)
    return jnp.einsum('bse,ed->bsd', merge_heads(o).astype(h.dtype), w_out)


def swiglu(h, w_gate, w_up, w_down):
    a = jnp.einsum('bsd,df->bsf', h, w_gate)
    u = jnp.einsum('bsd,df->bsf', h, w_up)
    return jnp.einsum('bsf,fd->bsd', jax.nn.silu(a) * u, w_down)


def setup_inputs(seed: int = 0) -> dict:
    key = jax.random.key(seed)
    ks = jax.random.split(key, 24)
    f32 = jnp.float32

    def dense(k, shape, fan_in):
        return jax.random.normal(k, shape, f32) * fan_in ** -0.5

    def gain(k, shape):
        return 1.0 + 0.02 * jax.random.normal(k, shape, f32)

    return {
        "x": jax.random.normal(ks[0], (BATCH, SEQ, D_MODEL), f32),
        "attn_norm_g": gain(ks[1], (DEPTH, D_MODEL)),
        "ffn_norm_g": gain(ks[2], (DEPTH, D_MODEL)),
        "even_w_in": dense(ks[3], (N_EVEN, D_MODEL, EVEN_IN), D_MODEL),
        "fox_forget_b": 3.0 + 0.5 * jax.random.normal(ks[4], (N_EVEN, N_HEADS_FOX), f32),
        "fox_q_norm_g": gain(ks[5], (N_EVEN, HEAD_DIM)),
        "fox_k_norm_g": gain(ks[6], (N_EVEN, HEAD_DIM)),
        "even_w_out": dense(ks[7], (N_EVEN, EVEN_MIX, D_MODEL), EVEN_MIX),
        "diff_w_in": dense(ks[8], (N_ODD, D_MODEL, DIFF_IN), D_MODEL),
        "diff_q_norm_g": gain(ks[9], (N_ODD, HEAD_DIM)),
        "diff_k_norm_g": gain(ks[10], (N_ODD, HEAD_DIM)),
        "diff_lambda_q1": 0.1 * jax.random.normal(ks[11], (N_ODD, HEAD_DIM), f32),
        "diff_lambda_k1": 0.1 * jax.random.normal(ks[12], (N_ODD, HEAD_DIM), f32),
        "diff_lambda_q2": 0.1 * jax.random.normal(ks[13], (N_ODD, HEAD_DIM), f32),
        "diff_lambda_k2": 0.1 * jax.random.normal(ks[14], (N_ODD, HEAD_DIM), f32),
        "diff_subln_g": gain(ks[15], (N_ODD, DIFF_V_DIM)),
        "diff_w_out": dense(ks[16], (N_ODD, DIFF_V, D_MODEL), DIFF_V),
        "rel_bias": 0.5 * jax.random.normal(ks[17], (N_BUCKETS, N_HEADS_DIFF), f32),
        "ffn_w_gate": dense(ks[18], (DEPTH, D_MODEL, D_FF), D_MODEL),
        "ffn_w_up": dense(ks[19], (DEPTH, D_MODEL, D_FF), D_MODEL),
        "ffn_w_down": dense(ks[20], (DEPTH, D_FF, D_MODEL), D_FF),
    }


def reference(x, attn_norm_g, ffn_norm_g, even_w_in, fox_forget_b, fox_q_norm_g, fox_k_norm_g,
              even_w_out, diff_w_in, diff_q_norm_g, diff_k_norm_g, diff_lambda_q1, diff_lambda_k1,
              diff_lambda_q2, diff_lambda_k2, diff_subln_g, diff_w_out, rel_bias,
              ffn_w_gate, ffn_w_up, ffn_w_down):
    for layer in range(DEPTH):
        h = rms_norm(x, attn_norm_g[layer]).astype(x.dtype)
        if layer % 2 == 0:
            e = layer // 2
            mix = even_mixer(h, even_w_in[e], fox_forget_b[e], fox_q_norm_g[e], fox_k_norm_g[e],
                             even_w_out[e])
        else:
            o = layer // 2
            mix = diff_mixer(h, diff_w_in[o], diff_q_norm_g[o], diff_k_norm_g[o],
                             diff_lambda_q1[o], diff_lambda_k1[o], diff_lambda_q2[o],
                             diff_lambda_k2[o], diff_subln_g[o], diff_w_out[o], rel_bias, layer)
        x = x + mix.astype(x.dtype)
        h = rms_norm(x, ffn_norm_g[layer]).astype(x.dtype)
        x = x + swiglu(h, ffn_w_gate[layer], ffn_w_up[layer], ffn_w_down[layer]).astype(x.dtype)
    return x
```

```python
import functools
import math

import jax
import jax.numpy as jnp
from jax import lax
from jax.experimental import pallas as pl
from jax.experimental.pallas import tpu as pltpu

F32 = jnp.float32
BF16 = jnp.bfloat16

HEAD_DIM = 64
N_BUCKETS = 32
MAX_DISTANCE = 128
RMS_EPS = 1e-6
LOG2E = 1.4426950408889634
NEG_BIG = -1e30

LANES = 128
ATT_TILE = 256
TOKEN_TILE = 512
PROJ_COL_TILE = 512
FFN_COL_TILE = 512
ONES_ROWS = 16
VMEM_LIMIT = 56 * 1024 * 1024


def _dot(a, b):
    return jnp.dot(a, b, preferred_element_type=F32)


def _params(n_axes, vmem=VMEM_LIMIT):
    return pltpu.CompilerParams(dimension_semantics=("arbitrary",) * n_axes, vmem_limit_bytes=vmem)


def _rms_rows(x, g):
    ms = jnp.mean(x * x, axis=-1, keepdims=True)
    return x * lax.rsqrt(ms + RMS_EPS) * g


def _split3(x):
    hi = x.astype(BF16)
    r = x - hi.astype(F32)
    mid = r.astype(BF16)
    lo = (r - mid.astype(F32)).astype(BF16)
    return hi, mid, lo


def _pre_body(x_ref, g_ref, w_ref, o_ref):
    h = _rms_rows(x_ref[0], g_ref[...]).astype(BF16)
    n = w_ref.shape[1]
    for j in range(n // PROJ_COL_TILE):
        cols = slice(j * PROJ_COL_TILE, (j + 1) * PROJ_COL_TILE)
        o_ref[0, :, cols] = _dot(h, w_ref[:, cols]).astype(o_ref.dtype)
    return h


def _pre_odd_kernel(x_ref, g_ref, w_ref, o_ref):
    _pre_body(x_ref, g_ref, w_ref, o_ref)


def _pre_even_kernel(x_ref, g_ref, w_ref, wfg_ref, fb_ref, ltri_ref, o_ref, c_ref, carry_ref):
    h = _pre_body(x_ref, g_ref, w_ref, o_ref)

    @pl.when(pl.program_id(1) == 0)
    def _():
        carry_ref[...] = jnp.zeros_like(carry_ref)

    fg = _dot(h, wfg_ref[...]) + fb_ref[...]
    logf = jnp.minimum(fg, 0.0) - jnp.log1p(jnp.exp(-jnp.abs(fg)))
    tc = ltri_ref.shape[0]
    carry = carry_ref[...]
    for ci in range(logf.shape[0] // tc):
        rows = slice(ci * tc, (ci + 1) * tc)
        hi, mid, lo = _split3(logf[rows])
        cs = _dot(ltri_ref[...], hi) + _dot(ltri_ref[...], mid) + _dot(ltri_ref[...], lo) + carry
        c_ref[0, rows, :] = cs
        carry = cs[tc - 1:tc, :]
    carry_ref[...] = carry


def _pre_call(x, g, w, extra=None):
    b, s, d = x.shape
    n = w.shape[1]
    tm = min(TOKEN_TILE, s)
    grid = (b, s // tm)
    x_spec = pl.BlockSpec((1, tm, d), lambda i, j: (i, j, 0))
    g_spec = pl.BlockSpec((1, d), lambda i, j: (0, 0))
    w_spec = pl.BlockSpec((d, n), lambda i, j: (0, 0))
    o_spec = pl.BlockSpec((1, tm, n), lambda i, j: (i, j, 0))
    o_shape = jax.ShapeDtypeStruct((b, s, n), BF16)
    if extra is None:
        return pl.pallas_call(
            _pre_odd_kernel, grid=grid, in_specs=[x_spec, g_spec, w_spec], out_specs=o_spec,
            out_shape=o_shape, compiler_params=_params(2), name="pre_odd")(x, g, w)
    wfg, fb = extra
    tc = min(ATT_TILE, tm)
    ltri = (lax.broadcasted_iota(jnp.int32, (tc, tc), 0) >= lax.broadcasted_iota(jnp.int32, (tc, tc), 1)).astype(BF16)
    return pl.pallas_call(
        _pre_even_kernel, grid=grid,
        in_specs=[x_spec, g_spec, w_spec,
                  pl.BlockSpec((d, LANES), lambda i, j: (0, 0)),
                  pl.BlockSpec((1, LANES), lambda i, j: (0, 0)),
                  pl.BlockSpec((tc, tc), lambda i, j: (0, 0))],
        out_specs=[o_spec, pl.BlockSpec((1, tm, LANES), lambda i, j: (i, j, 0))],
        out_shape=[o_shape, jax.ShapeDtypeStruct((b, s, LANES), F32)],
        scratch_shapes=[pltpu.VMEM((1, LANES), F32)],
        compiler_params=_params(2), name="pre_even")(x, g, w, wfg, fb, ltri)


def _tile_iotas(t):
    return (lax.broadcasted_iota(jnp.int32, (t, t), 0), lax.broadcasted_iota(jnp.int32, (t, t), 1))


def _store_split_qt(qn, qat_ref, qbt_ref):
    qt = qn.T
    row = lax.broadcasted_iota(jnp.int32, qt.shape, 0)
    qat_ref[...] = jnp.where(row < HEAD_DIM, qt, 0.0).astype(BF16)
    qbt_ref[...] = jnp.where(row >= HEAD_DIM, qt, 0.0).astype(BF16)


def _group_rms(x, bd, g):
    ss = _dot((x * x).astype(BF16), bd)
    return x * lax.rsqrt(ss * (1.0 / HEAD_DIM) + RMS_EPS) * g


def _softmax_tile(z, m_ref, row, acc_ref, vt):
    m_old = m_ref[row:row + 1, :]
    m_new = jnp.maximum(m_old, jnp.max(z, axis=0, keepdims=True))
    p = jnp.exp2(z - m_new).astype(BF16)
    acc_ref[...] = acc_ref[...] * jnp.exp2(m_old - m_new) + _dot(vt, p)
    m_ref[row:row + 1, :] = m_new


def _sb_kernel(q_ref, k_ref, v_ref, u_ref, o_ref, vt_ref, qat_ref, qbt_ref, acc_ref, r_ref):
    t = q_ref.shape[1]
    qi = pl.program_id(2)
    nblk = k_ref.shape[1] // t

    @pl.when(qi == 0)
    def _():
        for c in range(nblk):
            vt_ref[c] = v_ref[0, c * t:(c + 1) * t, :].astype(F32).T.astype(BF16)

    _store_split_qt(q_ref[0].astype(F32) * (HEAD_DIM ** -0.5 * LOG2E), qat_ref, qbt_ref)
    acc_ref[...] = jnp.zeros_like(acc_ref)
    r_ref[...] = jnp.zeros_like(r_ref)
    krow, qcol = _tile_iotas(t)
    strict = krow < qcol

    def tile(j, masked):
        start = pl.multiple_of(j * t, t)
        kblk = k_ref[0, pl.ds(start, t), :]
        for sub, qt_ref in enumerate((qat_ref, qbt_ref)):
            rows = slice(sub * HEAD_DIM, (sub + 1) * HEAD_DIM)
            z = _dot(kblk, qt_ref[...])
            sp = jnp.maximum(z, 0.0) + LOG2E * jnp.log(1.0 + jnp.exp2(-jnp.abs(z)))
            if masked:
                sp = jnp.where(strict, sp, 0.0)
            hi = sp.astype(BF16)
            lo = (sp - hi.astype(F32)).astype(BF16)
            cum = _dot(u_ref[...], hi) + _dot(u_ref[...], lo)
            r_old = r_ref[sub:sub + 1, :]
            w = jnp.exp2(z - sp - cum - r_old)
            if masked:
                w = jnp.where(strict, w, 0.0)
            acc_ref[rows, :] += _dot(vt_ref[j, rows, :], w.astype(BF16))
            r_ref[sub:sub + 1, :] = r_old + cum[0:1, :] + sp[0:1, :]

    tile(qi, True)

    def body(jj, carry):
        tile(qi - 1 - jj, False)
        return carry

    lax.fori_loop(0, qi, body, 0)
    o_ref[0] = acc_ref[...].T.astype(o_ref.dtype)


def _sb_call(proj, n_pairs, q_col, k_col, v_col):
    b, s, _ = proj.shape
    t = min(ATT_TILE, s)
    nblk = s // t
    u = (lax.broadcasted_iota(jnp.int32, (t, t), 1) > lax.broadcasted_iota(jnp.int32, (t, t), 0)).astype(BF16)
    return pl.pallas_call(
        _sb_kernel, grid=(b, n_pairs, nblk),
        in_specs=[pl.BlockSpec((1, t, LANES), lambda i, p, q: (i, q, q_col + p)),
                  pl.BlockSpec((1, s, LANES), lambda i, p, q: (i, 0, k_col + p)),
                  pl.BlockSpec((1, s, LANES), lambda i, p, q: (i, 0, v_col + p)),
                  pl.BlockSpec((t, t), lambda i, p, q: (0, 0))],
        out_specs=pl.BlockSpec((1, t, LANES), lambda i, p, q: (i, q, p)),
        out_shape=jax.ShapeDtypeStruct((b, s, n_pairs * LANES), BF16),
        scratch_shapes=[pltpu.VMEM((nblk, LANES, t), BF16),
                        pltpu.VMEM((LANES, t), BF16), pltpu.VMEM((LANES, t), BF16),
                        pltpu.VMEM((LANES, t), F32), pltpu.VMEM((8, t), F32)],
        compiler_params=_params(3), name="sb_attn")(proj, proj, proj, u)


def _fox_kernel(q_ref, k_ref, v_ref, c_ref, gq_ref, gk_ref, bd_ref, o_ref,
                kn_ref, vta_ref, vtb_ref, cb_ref, qat_ref, qbt_ref, acca_ref, accb_ref, m_ref):
    t = q_ref.shape[1]
    p_idx = pl.program_id(1)
    qi = pl.program_id(2)
    nblk = k_ref.shape[1] // t

    @pl.when(qi == 0)
    def _():
        sel_r = lax.broadcasted_iota(jnp.int32, (LANES, 2 * LANES), 0)
        sel_l = lax.broadcasted_iota(jnp.int32, (LANES, 2 * LANES), 1)
        sel = (sel_r == 2 * p_idx + (sel_l >= LANES).astype(jnp.int32)).astype(BF16)
        ones = jnp.ones((ONES_ROWS, t), BF16)
        for c in range(nblk):
            rows = slice(c * t, (c + 1) * t)
            kn_ref[rows, :] = _group_rms(k_ref[0, rows, :].astype(F32), bd_ref[...], gk_ref[...]).astype(BF16)
            vt = v_ref[0, rows, :].astype(F32).T
            vta_ref[c, 0:HEAD_DIM, :] = vt[0:HEAD_DIM].astype(BF16)
            vtb_ref[c, 0:HEAD_DIM, :] = vt[HEAD_DIM:].astype(BF16)
            vta_ref[c, HEAD_DIM:, :] = ones
            vtb_ref[c, HEAD_DIM:, :] = ones
            hi, mid, lo = _split3(c_ref[0, rows, :] * LOG2E)
            cb = _dot(hi, sel) + _dot(mid, sel) + _dot(lo, sel)
            cb_ref[0, c] = cb[:, :LANES]
            cb_ref[1, c] = cb[:, LANES:]

    qn = _group_rms(q_ref[0].astype(F32), bd_ref[...], gq_ref[...]) * (HEAD_DIM ** -0.5 * LOG2E)
    _store_split_qt(qn, qat_ref, qbt_ref)
    acca_ref[...] = jnp.zeros_like(acca_ref)
    accb_ref[...] = jnp.zeros_like(accb_ref)
    m_ref[...] = jnp.full_like(m_ref, NEG_BIG)
    krow, qcol = _tile_iotas(t)
    causal = krow <= qcol

    def tile(j, masked):
        start = pl.multiple_of(j * t, t)
        kblk = kn_ref[pl.ds(start, t), :]
        for sub, (qt_ref, acc_ref, vt_ref) in enumerate(((qat_ref, acca_ref, vta_ref), (qbt_ref, accb_ref, vtb_ref))):
            cb = cb_ref[sub, j]
            z = _dot(kblk, qt_ref[...]) - jnp.concatenate([cb] * (t // LANES), axis=1)
            if masked:
                z = jnp.where(causal, z, NEG_BIG)
            _softmax_tile(z, m_ref, sub, acc_ref, vt_ref[j])

    tile(qi, True)

    def body(jj, carry):
        tile(qi - 1 - jj, False)
        return carry

    lax.fori_loop(0, qi, body, 0)
    oa = acca_ref[0:HEAD_DIM, :] / acca_ref[HEAD_DIM:HEAD_DIM + 1, :]
    ob = accb_ref[0:HEAD_DIM, :] / accb_ref[HEAD_DIM:HEAD_DIM + 1, :]
    o_ref[0] = jnp.concatenate([oa, ob], axis=0).T.astype(o_ref.dtype)


def _fox_call(proj, c, gq, gk, n_pairs, q_col, k_col, v_col):
    b, s, _ = proj.shape
    t = min(ATT_TILE, s)
    nblk = s // t
    blk = lax.broadcasted_iota(jnp.int32, (LANES, LANES), 0) // HEAD_DIM
    bd = (blk == lax.broadcasted_iota(jnp.int32, (LANES, LANES), 1) // HEAD_DIM).astype(BF16)
    const = lambda shape: pl.BlockSpec(shape, lambda i, p, q: (0,) * len(shape))
    rows_aug = HEAD_DIM + ONES_ROWS
    return pl.pallas_call(
        _fox_kernel, grid=(b, n_pairs, nblk),
        in_specs=[pl.BlockSpec((1, t, LANES), lambda i, p, q: (i, q, q_col + p)),
                  pl.BlockSpec((1, s, LANES), lambda i, p, q: (i, 0, k_col + p)),
                  pl.BlockSpec((1, s, LANES), lambda i, p, q: (i, 0, v_col + p)),
                  pl.BlockSpec((1, s, LANES), lambda i, p, q: (i, 0, 0)),
                  const((1, LANES)), const((1, LANES)), const((LANES, LANES))],
        out_specs=pl.BlockSpec((1, t, LANES), lambda i, p, q: (i, q, p)),
        out_shape=jax.ShapeDtypeStruct((b, s, n_pairs * LANES), BF16),
        scratch_shapes=[pltpu.VMEM((s, LANES), BF16),
                        pltpu.VMEM((nblk, rows_aug, t), BF16), pltpu.VMEM((nblk, rows_aug, t), BF16),
                        pltpu.VMEM((2, nblk, t, LANES), F32),
                        pltpu.VMEM((LANES, t), BF16), pltpu.VMEM((LANES, t), BF16),
                        pltpu.VMEM((rows_aug, t), F32), pltpu.VMEM((rows_aug, t), F32),
                        pltpu.VMEM((8, t), F32)],
        compiler_params=_params(3), name="fox_attn")(proj, proj, proj, c, gq, gk, bd)


def _diff_kernel(q_ref, k_ref, v_ref, bias_ref, gq_ref, gk_ref, bd_ref, lam_ref, sg_ref, o_ref,
                 kn_ref, vt_ref, q1t_ref, q2t_ref, acc1_ref, acc2_ref, m_ref, *, lam_init):
    t = q_ref.shape[1]
    qi = pl.program_id(2)
    nblk = k_ref.shape[1] // t
    dv = v_ref.shape[2]

    @pl.when(qi == 0)
    def _():
        ones = jnp.ones((ONES_ROWS, t), BF16)
        for c in range(nblk):
            rows = slice(c * t, (c + 1) * t)
            kn_ref[rows, :] = _group_rms(k_ref[0, rows, :].astype(F32), bd_ref[...], gk_ref[...]).astype(BF16)
            vt_ref[c, 0:dv, :] = v_ref[0, rows, :].astype(F32).T.astype(BF16)
            vt_ref[c, dv:, :] = ones

    qn = _group_rms(q_ref[0].astype(F32), bd_ref[...], gq_ref[...]) * (HEAD_DIM ** -0.5 * LOG2E)
    _store_split_qt(qn, q1t_ref, q2t_ref)
    acc1_ref[...] = jnp.zeros_like(acc1_ref)
    acc2_ref[...] = jnp.zeros_like(acc2_ref)
    m_ref[...] = jnp.full_like(m_ref, NEG_BIG)
    krow, qcol = _tile_iotas(t)
    causal = krow <= qcol

    def tile(j, offset):
        start = pl.multiple_of(j * t, t)
        kblk = kn_ref[pl.ds(start, t), :]
        for sub, (qt_ref, acc_ref) in enumerate(((q1t_ref, acc1_ref), (q2t_ref, acc2_ref))):
            z = _dot(kblk, qt_ref[...])
            if offset is not None:
                z = z + bias_ref[0, offset]
            if offset == 0:
                z = jnp.where(causal, z, NEG_BIG)
            _softmax_tile(z, m_ref, sub, acc_ref, vt_ref[j])

    tile(qi, 0)

    @pl.when(qi >= 1)
    def _():
        tile(qi - 1, 1)

    def body(jj, carry):
        tile(qi - 2 - jj, None)
        return carry

    lax.fori_loop(0, jnp.maximum(qi - 1, 0), body, 0)

    lam = (jnp.exp(jnp.sum(lam_ref[0:1, :] * lam_ref[1:2, :], axis=-1, keepdims=True))
           - jnp.exp(jnp.sum(lam_ref[2:3, :] * lam_ref[3:4, :], axis=-1, keepdims=True)) + lam_init)
    o1 = acc1_ref[0:dv, :] / acc1_ref[dv:dv + 1, :]
    o2 = acc2_ref[0:dv, :] / acc2_ref[dv:dv + 1, :]
    o = o1 - lam * o2
    o = o * lax.rsqrt(jnp.mean(o * o, axis=0, keepdims=True) + RMS_EPS)
    o_ref[0] = (o.T * sg_ref[...] * (1.0 - lam_init)).astype(o_ref.dtype)


def _diff_call(proj, bias, gq, gk, lam_rows, subln_g, n_heads, lam_init):
    b, s, _ = proj.shape
    t = min(ATT_TILE, s)
    nblk = s // t
    blk = lax.broadcasted_iota(jnp.int32, (LANES, LANES), 0) // HEAD_DIM
    bd = (blk == lax.broadcasted_iota(jnp.int32, (LANES, LANES), 1) // HEAD_DIM).astype(BF16)
    const = lambda shape: pl.BlockSpec(shape, lambda i, p, q: (0,) * len(shape))
    dv = 2 * HEAD_DIM
    rows_aug = dv + ONES_ROWS
    return pl.pallas_call(
        functools.partial(_diff_kernel, lam_init=lam_init), grid=(b, n_heads, nblk),
        in_specs=[pl.BlockSpec((1, t, LANES), lambda i, p, q: (i, q, p)),
                  pl.BlockSpec((1, s, LANES), lambda i, p, q: (i, 0, n_heads + p)),
                  pl.BlockSpec((1, s, dv), lambda i, p, q: (i, 0, 2 * n_heads + p)),
                  pl.BlockSpec((1, 2, t, t), lambda i, p, q: (p, 0, 0, 0)),
                  const((1, LANES)), const((1, LANES)), const((LANES, LANES)), const((8, LANES)), const((1, dv))],
        out_specs=pl.BlockSpec((1, t, dv), lambda i, p, q: (i, q, p)),
        out_shape=jax.ShapeDtypeStruct((b, s, n_heads * dv), BF16),
        scratch_shapes=[pltpu.VMEM((s, LANES), BF16),
                        pltpu.VMEM((nblk, rows_aug, t), BF16),
                        pltpu.VMEM((LANES, t), BF16), pltpu.VMEM((LANES, t), BF16),
                        pltpu.VMEM((rows_aug, t), F32), pltpu.VMEM((rows_aug, t), F32),
                        pltpu.VMEM((8, t), F32)],
        compiler_params=_params(3), name="diff_attn")(proj, proj, proj, bias, gq, gk, bd, lam_rows, subln_g)


def _t5_bucket(dist):
    max_exact = N_BUCKETS // 2
    nf = jnp.maximum(dist, 1).astype(F32)
    large = max_exact + (jnp.log(nf / max_exact) / math.log(MAX_DISTANCE / max_exact)
                         * (N_BUCKETS - max_exact)).astype(jnp.int32)
    large = jnp.minimum(large, N_BUCKETS - 1)
    return jnp.where(dist < max_exact, dist, large)


def _bias_tiles(rel_bias, t):
    assert t >= MAX_DISTANCE
    table = rel_bias.astype(F32)
    krow = lax.broadcasted_iota(jnp.int32, (2, t, t), 1)
    qcol = lax.broadcasted_iota(jnp.int32, (2, t, t), 2)
    dist = qcol - krow + t * lax.broadcasted_iota(jnp.int32, (2, t, t), 0)
    bias = table[_t5_bucket(jnp.maximum(dist, 0))] - table[N_BUCKETS - 1]
    return jnp.transpose(bias, (3, 0, 1, 2)) * LOG2E


def _post_kernel(*refs, n_mix):
    x_ref = refs[0]
    o_refs = refs[1:1 + n_mix]
    wo_ref, g_ref, wg_ref, wu_ref, wd_ref, out_ref, x1_ref, act_ref = refs[1 + n_mix:]
    o = jnp.concatenate([o_ref[0] for o_ref in o_refs], axis=1) if n_mix > 1 else o_refs[0][0]
    d = x_ref.shape[2]
    for lo in range(0, d, PROJ_COL_TILE):
        cols = slice(lo, lo + PROJ_COL_TILE)
        x1_ref[:, cols] = x_ref[0, :, cols] + _dot(o, wo_ref[:, cols])
    h = _rms_rows(x1_ref[...], g_ref[...]).astype(BF16)
    d_ff = wg_ref.shape[1]
    for lo in range(0, d_ff, FFN_COL_TILE):
        cols = slice(lo, min(lo + FFN_COL_TILE, d_ff))
        a = _dot(h, wg_ref[:, cols])
        u = _dot(h, wu_ref[:, cols])
        act_ref[:, cols] = (a * (1.0 / (1.0 + jnp.exp(-a))) * u).astype(BF16)
    for lo in range(0, d, PROJ_COL_TILE):
        cols = slice(lo, lo + PROJ_COL_TILE)
        out_ref[0, :, cols] = x1_ref[:, cols] + _dot(act_ref[...], wd_ref[:, cols])


def _post_call(x, mixes, w_out, g, wg, wu, wd):
    b, s, d = x.shape
    tm = min(TOKEN_TILE, s)
    n_mix = len(mixes)
    tok = lambda width: pl.BlockSpec((1, tm, width), lambda i, j: (i, j, 0))
    res = lambda shape: pl.BlockSpec(shape, lambda i, j: (0, 0), pipeline_mode=pl.Buffered(1))
    return pl.pallas_call(
        functools.partial(_post_kernel, n_mix=n_mix), grid=(b, s // tm),
        in_specs=([tok(d)] + [tok(m.shape[2]) for m in mixes]
                  + [res(w_out.shape), res((1, d)), res(wg.shape), res(wu.shape), res(wd.shape)]),
        out_specs=tok(d),
        out_shape=jax.ShapeDtypeStruct((b, s, d), x.dtype),
        scratch_shapes=[pltpu.VMEM((tm, d), F32), pltpu.VMEM((tm, wg.shape[1]), BF16)],
        compiler_params=_params(2), name="post")(x, *mixes, w_out, g, wg, wu, wd)


def _pair_gain(g):
    return jnp.concatenate([g, g]).astype(F32)[None, :]


def kernel(x, attn_norm_g, ffn_norm_g, even_w_in, fox_forget_b, fox_q_norm_g, fox_k_norm_g, even_w_out, diff_w_in, diff_q_norm_g, diff_k_norm_g, diff_lambda_q1, diff_lambda_k1, diff_lambda_q2, diff_lambda_k2, diff_subln_g, diff_w_out, rel_bias, ffn_w_gate, ffn_w_up, ffn_w_down):
    b, s, d = x.shape
    depth = attn_norm_g.shape[0]
    n_sb = d // (2 * HEAD_DIM)
    n_fox = d // (2 * HEAD_DIM)
    n_diff = d // (2 * HEAD_DIM)
    sb_w = n_sb * HEAD_DIM
    fox_w = n_fox * HEAD_DIM
    main_w = 3 * sb_w + 3 * fox_w
    assert sb_w % LANES == 0 and fox_w % LANES == 0 and n_fox <= LANES and s % min(ATT_TILE, s) == 0
    bias = _bias_tiles(rel_bias, min(ATT_TILE, s))

    for layer in range(depth):
        g_attn = attn_norm_g[layer][None, :]
        if layer % 2 == 0:
            e = layer // 2
            w_in = even_w_in[e]
            wfg = jnp.pad(w_in[:, main_w:], ((0, 0), (0, LANES - n_fox))).astype(BF16)
            fb = jnp.pad(fox_forget_b[e].astype(F32), (0, LANES - n_fox))[None, :]
            proj, c = _pre_call(x, g_attn, w_in[:, :main_w].astype(BF16), (wfg, fb))
            sbc, fxc = sb_w // LANES, fox_w // LANES
            o_a = _sb_call(proj, sbc, 0, sbc, 2 * sbc)
            o_b = _fox_call(proj, c, _pair_gain(fox_q_norm_g[e]), _pair_gain(fox_k_norm_g[e]),
                            fxc, 3 * sbc, 3 * sbc + fxc, 3 * sbc + 2 * fxc)
            mixes, w_out = [o_a, o_b], even_w_out[e].astype(BF16)
        else:
            o = layer // 2
            proj = _pre_call(x, g_attn, diff_w_in[o].astype(BF16))
            lam_rows = jnp.stack([diff_lambda_q1[o], diff_lambda_k1[o], diff_lambda_q2[o], diff_lambda_k2[o]])
            lam_rows = jnp.pad(lam_rows.astype(F32), ((0, 4), (0, LANES - HEAD_DIM)))
            lam_init = 0.8 - 0.6 * math.exp(-0.3 * layer)
            mix = _diff_call(proj, bias, _pair_gain(diff_q_norm_g[o]), _pair_gain(diff_k_norm_g[o]),
                             lam_rows, diff_subln_g[o].astype(F32)[None, :], n_diff, lam_init)
            mixes, w_out = [mix], diff_w_out[o].astype(BF16)
        x = _post_call(x, mixes, w_out, ffn_norm_g[layer][None, :], ffn_w_gate[layer].astype(BF16),
                       ffn_w_up[layer].astype(BF16), ffn_w_down[layer].astype(BF16))
    return x
```

```python
import functools
import math

import jax
import jax.numpy as jnp
import numpy as np
from jax import lax
from jax.experimental import pallas as pl
from jax.experimental.pallas import tpu as pltpu

F32 = jnp.float32
BF16 = jnp.bfloat16

HEAD_DIM = 64
N_BUCKETS = 32
MAX_DISTANCE = 128
RMS_EPS = 1e-6
LOG2E = 1.4426950408889634
NEG_BIG = -1e30
SIGN_BIT = np.uint32(0x80000000)

LANES = 128
K_TILE = 256
Q_TILE = 1024
TOKEN_TILE = 512
PROJ_COL_TILE = 512
FFN_COL_TILE = 512
ONES_ROWS = 16
VMEM_LIMIT = 56 * 1024 * 1024


def _dot(a, b):
    return jnp.dot(a, b, preferred_element_type=F32)


def _params(n_axes, vmem=VMEM_LIMIT):
    return pltpu.CompilerParams(dimension_semantics=("arbitrary",) * n_axes, vmem_limit_bytes=vmem)


def _rms_rows(x, g):
    ms = jnp.mean(x * x, axis=-1, keepdims=True)
    return x * lax.rsqrt(ms + RMS_EPS) * g


def _split3(x):
    hi = x.astype(BF16)
    r = x - hi.astype(F32)
    mid = r.astype(BF16)
    lo = (r - mid.astype(F32)).astype(BF16)
    return hi, mid, lo


def _pre_body(x_ref, g_ref, w_ref, o_ref):
    h = _rms_rows(x_ref[0], g_ref[...]).astype(BF16)
    n = w_ref.shape[1]
    for j in range(n // PROJ_COL_TILE):
        cols = slice(j * PROJ_COL_TILE, (j + 1) * PROJ_COL_TILE)
        o_ref[0, :, cols] = _dot(h, w_ref[:, cols]).astype(o_ref.dtype)
    return h


def _pre_odd_kernel(x_ref, g_ref, w_ref, o_ref):
    _pre_body(x_ref, g_ref, w_ref, o_ref)


def _pre_even_kernel(x_ref, g_ref, w_ref, wfg_ref, fb_ref, ltri_ref, o_ref, c_ref, carry_ref):
    h = _pre_body(x_ref, g_ref, w_ref, o_ref)

    @pl.when(pl.program_id(1) == 0)
    def _():
        carry_ref[...] = jnp.zeros_like(carry_ref)

    fg = _dot(h, wfg_ref[...]) + fb_ref[...]
    logf = jnp.minimum(fg, 0.0) - jnp.log1p(jnp.exp(-jnp.abs(fg)))
    tc = ltri_ref.shape[0]
    carry = carry_ref[...]
    for ci in range(logf.shape[0] // tc):
        rows = slice(ci * tc, (ci + 1) * tc)
        hi, mid, lo = _split3(logf[rows])
        cs = _dot(ltri_ref[...], hi) + _dot(ltri_ref[...], mid) + _dot(ltri_ref[...], lo) + carry
        c_ref[0, rows, :] = cs
        carry = cs[tc - 1:tc, :]
    carry_ref[...] = carry


def _pre_call(x, g, w, extra=None):
    b, s, d = x.shape
    n = w.shape[1]
    tm = min(TOKEN_TILE, s)
    grid = (b, s // tm)
    x_spec = pl.BlockSpec((1, tm, d), lambda i, j: (i, j, 0))
    g_spec = pl.BlockSpec((1, d), lambda i, j: (0, 0))
    w_spec = pl.BlockSpec((d, n), lambda i, j: (0, 0))
    o_spec = pl.BlockSpec((1, tm, n), lambda i, j: (i, j, 0))
    o_shape = jax.ShapeDtypeStruct((b, s, n), BF16)
    if extra is None:
        return pl.pallas_call(
            _pre_odd_kernel, grid=grid, in_specs=[x_spec, g_spec, w_spec], out_specs=o_spec,
            out_shape=o_shape, compiler_params=_params(2), name="pre_odd")(x, g, w)
    wfg, fb = extra
    tc = min(K_TILE, tm)
    ltri = (lax.broadcasted_iota(jnp.int32, (tc, tc), 0) >= lax.broadcasted_iota(jnp.int32, (tc, tc), 1)).astype(BF16)
    return pl.pallas_call(
        _pre_even_kernel, grid=grid,
        in_specs=[x_spec, g_spec, w_spec,
                  pl.BlockSpec((d, LANES), lambda i, j: (0, 0)),
                  pl.BlockSpec((1, LANES), lambda i, j: (0, 0)),
                  pl.BlockSpec((tc, tc), lambda i, j: (0, 0))],
        out_specs=[o_spec, pl.BlockSpec((1, tm, LANES), lambda i, j: (i, j, 0))],
        out_shape=[o_shape, jax.ShapeDtypeStruct((b, s, LANES), F32)],
        scratch_shapes=[pltpu.VMEM((1, LANES), F32)],
        compiler_params=_params(2), name="pre_even")(x, g, w, wfg, fb, ltri)


def _attn_tiles(s):
    tk = min(K_TILE, s)
    tq = min(Q_TILE, s)
    assert s % tq == 0 and tq % tk == 0
    return tq, tk


def _first_block(x, fn, tk):
    if x.shape[1] == tk:
        return fn(x)
    return jnp.concatenate([fn(x[:, :tk]), x[:, tk:]], axis=1)


def _store_split_qt(qn, qat_ref, qbt_ref):
    tq = qn.shape[0]
    row = lax.broadcasted_iota(jnp.int32, (LANES, LANES), 0)
    for c in range(tq // LANES):
        cols = slice(c * LANES, (c + 1) * LANES)
        qt = qn[cols].T
        qat_ref[:, cols] = jnp.where(row < HEAD_DIM, qt, 0.0).astype(BF16)
        qbt_ref[:, cols] = jnp.where(row >= HEAD_DIM, qt, 0.0).astype(BF16)


def _store_out_t(o_t, o_ref):
    for c in range(o_t.shape[1] // LANES):
        cols = slice(c * LANES, (c + 1) * LANES)
        o_ref[0, cols, :] = o_t[:, cols].T.astype(o_ref.dtype)


def _group_rms(x, bd, g):
    ss = _dot((x * x).astype(BF16), bd)
    return x * lax.rsqrt(ss * (1.0 / HEAD_DIM) + RMS_EPS) * g


def _softmax_tile(zs, m_ref, acc_refs, vts, c0, tk):
    n = zs[0].shape[1]
    for cb in range(n // tk):
        for row, (z, acc_ref, vt) in enumerate(zip(zs, acc_refs, vts)):
            cols = slice(c0 + cb * tk, c0 + (cb + 1) * tk)
            zb = z[:, cb * tk:(cb + 1) * tk]
            m_old = m_ref[row:row + 1, cols]
            m_new = jnp.maximum(m_old, jnp.max(zb, axis=0, keepdims=True))
            p = jnp.exp2(zb - m_new).astype(BF16)
            acc_ref[:, cols] = acc_ref[:, cols] * jnp.exp2(m_old - m_new) + _dot(vt, p)
            m_ref[row:row + 1, cols] = m_new


def _sweep(qi, tq, tk, tile):
    ncb = tq // tk
    for d in reversed(range(ncb)):
        tile(qi * ncb + d, d * tk, True)

    def body(jj, carry):
        tile(qi * ncb - 1 - jj, 0, False)
        return carry

    lax.fori_loop(0, qi * ncb, body, 0)


def _sb_kernel(q_ref, k_ref, v_ref, u_ref, o_ref, vt_ref, qat_ref, qbt_ref, acc_ref, r_ref):
    tq = q_ref.shape[1]
    tk = u_ref.shape[0]
    qi = pl.program_id(2)
    nblk = k_ref.shape[1] // tk

    @pl.when(qi == 0)
    def _():
        for c in range(nblk):
            vt_ref[c] = v_ref[0, c * tk:(c + 1) * tk, :].astype(F32).T.astype(BF16)

    _store_split_qt(q_ref[0].astype(F32) * (HEAD_DIM ** -0.5 * LOG2E), qat_ref, qbt_ref)
    acc_ref[...] = jnp.zeros_like(acc_ref)
    r_ref[...] = jnp.zeros_like(r_ref)
    strict = lax.broadcasted_iota(jnp.int32, (tk, tk), 0) < lax.broadcasted_iota(jnp.int32, (tk, tk), 1)

    def tile(j, c0, diag):
        start = pl.multiple_of(j * tk, tk)
        kblk = k_ref[0, pl.ds(start, tk), :]
        zs = [_dot(kblk, qt_ref[:, c0:]) for qt_ref in (qat_ref, qbt_ref)]
        chains = []
        for cb in range((tq - c0) // tk):
            tri = diag and cb == 0
            for sub in range(2):
                z = zs[sub][:, cb * tk:(cb + 1) * tk]
                neg_abs = lax.bitcast_convert_type(lax.bitcast_convert_type(z, jnp.uint32) | SIGN_BIT, F32)
                sp = jnp.maximum(z, 0.0) + LOG2E * jnp.log(1.0 + jnp.exp2(neg_abs))
                if tri:
                    sp = jnp.where(strict, sp, 0.0)
                hi = sp.astype(BF16)
                lo = (sp - hi.astype(F32)).astype(BF16)
                cum = _dot(u_ref[...], hi) + _dot(u_ref[...], lo)
                chains.append((cb, sub, tri, z, cum))
        for cb, sub, tri, z, cum in chains:
            rows = slice(sub * HEAD_DIM, (sub + 1) * HEAD_DIM)
            cols = slice(c0 + cb * tk, c0 + (cb + 1) * tk)
            r_old = r_ref[sub:sub + 1, cols]
            w = jnp.exp2(z - cum - r_old)
            if tri:
                w = jnp.where(strict, w, 0.0)
            acc_ref[rows, cols] += _dot(vt_ref[j, rows, :], w.astype(BF16))
            r_ref[sub:sub + 1, cols] = r_old + cum[0:1, :]

    _sweep(qi, tq, tk, tile)
    _store_out_t(acc_ref[...], o_ref)


def _sb_call(proj, n_pairs, q_col, k_col, v_col):
    b, s, _ = proj.shape
    tq, tk = _attn_tiles(s)
    nblk = s // tk
    u = (lax.broadcasted_iota(jnp.int32, (tk, tk), 1) >= lax.broadcasted_iota(jnp.int32, (tk, tk), 0)).astype(BF16)
    return pl.pallas_call(
        _sb_kernel, grid=(b, n_pairs, s // tq),
        in_specs=[pl.BlockSpec((1, tq, LANES), lambda i, p, q: (i, q, q_col + p)),
                  pl.BlockSpec((1, s, LANES), lambda i, p, q: (i, 0, k_col + p)),
                  pl.BlockSpec((1, s, LANES), lambda i, p, q: (i, 0, v_col + p)),
                  pl.BlockSpec((tk, tk), lambda i, p, q: (0, 0))],
        out_specs=pl.BlockSpec((1, tq, LANES), lambda i, p, q: (i, q, p)),
        out_shape=jax.ShapeDtypeStruct((b, s, n_pairs * LANES), BF16),
        scratch_shapes=[pltpu.VMEM((nblk, LANES, tk), BF16),
                        pltpu.VMEM((LANES, tq), BF16), pltpu.VMEM((LANES, tq), BF16),
                        pltpu.VMEM((LANES, tq), F32), pltpu.VMEM((8, tq), F32)],
        compiler_params=_params(3), name="sb_attn")(proj, proj, proj, u)


def _fox_kernel(q_ref, k_ref, v_ref, c_ref, gq_ref, gk_ref, bd_ref, o_ref,
                kn_ref, vta_ref, vtb_ref, cb_ref, qat_ref, qbt_ref, acca_ref, accb_ref, m_ref):
    tq = q_ref.shape[1]
    tk = vta_ref.shape[2]
    p_idx = pl.program_id(1)
    qi = pl.program_id(2)
    nblk = k_ref.shape[1] // tk

    @pl.when(qi == 0)
    def _():
        sel_r = lax.broadcasted_iota(jnp.int32, (LANES, 2 * LANES), 0)
        sel_l = lax.broadcasted_iota(jnp.int32, (LANES, 2 * LANES), 1)
        sel = (sel_r == 2 * p_idx + (sel_l >= LANES).astype(jnp.int32)).astype(BF16)
        ones = jnp.ones((ONES_ROWS, tk), BF16)
        for c in range(nblk):
            rows = slice(c * tk, (c + 1) * tk)
            kn_ref[rows, :] = _group_rms(k_ref[0, rows, :].astype(F32), bd_ref[...], gk_ref[...]).astype(BF16)
            vt = v_ref[0, rows, :].astype(F32).T
            vta_ref[c, 0:HEAD_DIM, :] = vt[0:HEAD_DIM].astype(BF16)
            vtb_ref[c, 0:HEAD_DIM, :] = vt[HEAD_DIM:].astype(BF16)
            vta_ref[c, HEAD_DIM:, :] = ones
            vtb_ref[c, HEAD_DIM:, :] = ones
            hi, mid, lo = _split3(c_ref[0, rows, :] * LOG2E)
            cb = _dot(hi, sel) + _dot(mid, sel) + _dot(lo, sel)
            cb_ref[0, c] = cb[:, :LANES]
            cb_ref[1, c] = cb[:, LANES:]

    for c in range(tq // tk):
        rows = slice(c * tk, (c + 1) * tk)
        qn = _group_rms(q_ref[0, rows, :].astype(F32), bd_ref[...], gq_ref[...]) * (HEAD_DIM ** -0.5 * LOG2E)
        _store_split_qt(qn, qat_ref.at[:, rows], qbt_ref.at[:, rows])
    acca_ref[...] = jnp.zeros_like(acca_ref)
    accb_ref[...] = jnp.zeros_like(accb_ref)
    m_ref[...] = jnp.full_like(m_ref, NEG_BIG)
    causal = lax.broadcasted_iota(jnp.int32, (tk, tk), 0) <= lax.broadcasted_iota(jnp.int32, (tk, tk), 1)

    def tile(j, c0, diag):
        start = pl.multiple_of(j * tk, tk)
        kblk = kn_ref[pl.ds(start, tk), :]
        zs = []
        for sub, qt_ref in enumerate((qat_ref, qbt_ref)):
            cb = cb_ref[sub, j]
            z = _dot(kblk, qt_ref[:, c0:]) - jnp.concatenate([cb] * ((tq - c0) // LANES), axis=1)
            if diag:
                z = _first_block(z, lambda a: jnp.where(causal, a, NEG_BIG), tk)
            zs.append(z)
        _softmax_tile(zs, m_ref, (acca_ref, accb_ref), (vta_ref[j], vtb_ref[j]), c0, tk)

    _sweep(qi, tq, tk, tile)
    oa = acca_ref[0:HEAD_DIM, :] / acca_ref[HEAD_DIM:HEAD_DIM + 1, :]
    ob = accb_ref[0:HEAD_DIM, :] / accb_ref[HEAD_DIM:HEAD_DIM + 1, :]
    _store_out_t(jnp.concatenate([oa, ob], axis=0), o_ref)


def _fox_call(proj, c, gq, gk, n_pairs, q_col, k_col, v_col):
    b, s, _ = proj.shape
    tq, tk = _attn_tiles(s)
    nblk = s // tk
    blk = lax.broadcasted_iota(jnp.int32, (LANES, LANES), 0) // HEAD_DIM
    bd = (blk == lax.broadcasted_iota(jnp.int32, (LANES, LANES), 1) // HEAD_DIM).astype(BF16)
    const = lambda shape: pl.BlockSpec(shape, lambda i, p, q: (0,) * len(shape))
    rows_aug = HEAD_DIM + ONES_ROWS
    return pl.pallas_call(
        _fox_kernel, grid=(b, n_pairs, s // tq),
        in_specs=[pl.BlockSpec((1, tq, LANES), lambda i, p, q: (i, q, q_col + p)),
                  pl.BlockSpec((1, s, LANES), lambda i, p, q: (i, 0, k_col + p)),
                  pl.BlockSpec((1, s, LANES), lambda i, p, q: (i, 0, v_col + p)),
                  pl.BlockSpec((1, s, LANES), lambda i, p, q: (i, 0, 0)),
                  const((1, LANES)), const((1, LANES)), const((LANES, LANES))],
        out_specs=pl.BlockSpec((1, tq, LANES), lambda i, p, q: (i, q, p)),
        out_shape=jax.ShapeDtypeStruct((b, s, n_pairs * LANES), BF16),
        scratch_shapes=[pltpu.VMEM((s, LANES), BF16),
                        pltpu.VMEM((nblk, rows_aug, tk), BF16), pltpu.VMEM((nblk, rows_aug, tk), BF16),
                        pltpu.VMEM((2, nblk, tk, LANES), F32),
                        pltpu.VMEM((LANES, tq), BF16), pltpu.VMEM((LANES, tq), BF16),
                        pltpu.VMEM((rows_aug, tq), F32), pltpu.VMEM((rows_aug, tq), F32),
                        pltpu.VMEM((8, tq), F32)],
        compiler_params=_params(3), name="fox_attn")(proj, proj, proj, c, gq, gk, bd)


def _diff_kernel(q_ref, k_ref, v_ref, bias_ref, gq_ref, gk_ref, bd_ref, lam_ref, sg_ref, o_ref,
                 kn_ref, vt_ref, q1t_ref, q2t_ref, acc1_ref, acc2_ref, m_ref, *, lam_init):
    tq = q_ref.shape[1]
    tk = vt_ref.shape[2]
    qi = pl.program_id(2)
    nblk = k_ref.shape[1] // tk
    ncb = tq // tk
    dv = v_ref.shape[2]

    @pl.when(qi == 0)
    def _():
        ones = jnp.ones((ONES_ROWS, tk), BF16)
        for c in range(nblk):
            rows = slice(c * tk, (c + 1) * tk)
            kn_ref[rows, :] = _group_rms(k_ref[0, rows, :].astype(F32), bd_ref[...], gk_ref[...]).astype(BF16)
            vt_ref[c, 0:dv, :] = v_ref[0, rows, :].astype(F32).T.astype(BF16)
            vt_ref[c, dv:, :] = ones

    for c in range(ncb):
        rows = slice(c * tk, (c + 1) * tk)
        qn = _group_rms(q_ref[0, rows, :].astype(F32), bd_ref[...], gq_ref[...]) * (HEAD_DIM ** -0.5 * LOG2E)
        _store_split_qt(qn, q1t_ref.at[:, rows], q2t_ref.at[:, rows])
    acc1_ref[...] = jnp.zeros_like(acc1_ref)
    acc2_ref[...] = jnp.zeros_like(acc2_ref)
    m_ref[...] = jnp.full_like(m_ref, NEG_BIG)
    causal = lax.broadcasted_iota(jnp.int32, (tk, tk), 0) <= lax.broadcasted_iota(jnp.int32, (tk, tk), 1)

    def tile(j, c0, diag, near=False):
        start = pl.multiple_of(j * tk, tk)
        kblk = kn_ref[pl.ds(start, tk), :]
        zs = []
        for qt_ref in (q1t_ref, q2t_ref):
            z = _dot(kblk, qt_ref[:, c0:])
            n = z.shape[1]
            if diag:
                parts = [jnp.where(causal, z[:, :tk] + bias_ref[0, 0], NEG_BIG)]
                if n > tk:
                    parts.append(z[:, tk:2 * tk] + bias_ref[0, 1])
                if n > 2 * tk:
                    parts.append(z[:, 2 * tk:])
                z = jnp.concatenate(parts, axis=1) if len(parts) > 1 else parts[0]
            elif near:
                z = _first_block(z, lambda a: a + bias_ref[0, 1], tk)
            zs.append(z)
        _softmax_tile(zs, m_ref, (acc1_ref, acc2_ref), (vt_ref[j], vt_ref[j]), c0, tk)

    for d in reversed(range(ncb)):
        tile(qi * ncb + d, d * tk, True)

    @pl.when(qi >= 1)
    def _():
        tile(qi * ncb - 1, 0, False, near=True)

    def body(jj, carry):
        tile(qi * ncb - 2 - jj, 0, False)
        return carry

    lax.fori_loop(0, jnp.maximum(qi * ncb - 1, 0), body, 0)

    lam = (jnp.exp(jnp.sum(lam_ref[0:1, :] * lam_ref[1:2, :], axis=-1, keepdims=True))
           - jnp.exp(jnp.sum(lam_ref[2:3, :] * lam_ref[3:4, :], axis=-1, keepdims=True)) + lam_init)
    o1 = acc1_ref[0:dv, :] / acc1_ref[dv:dv + 1, :]
    o2 = acc2_ref[0:dv, :] / acc2_ref[dv:dv + 1, :]
    o = o1 - lam * o2
    o = o * lax.rsqrt(jnp.mean(o * o, axis=0, keepdims=True) + RMS_EPS)
    for c in range(tq // LANES):
        cols = slice(c * LANES, (c + 1) * LANES)
        o_ref[0, cols, :] = (o[:, cols].T * sg_ref[...] * (1.0 - lam_init)).astype(o_ref.dtype)


def _diff_call(proj, bias, gq, gk, lam_rows, subln_g, n_heads, lam_init):
    b, s, _ = proj.shape
    tq, tk = _attn_tiles(s)
    nblk = s // tk
    blk = lax.broadcasted_iota(jnp.int32, (LANES, LANES), 0) // HEAD_DIM
    bd = (blk == lax.broadcasted_iota(jnp.int32, (LANES, LANES), 1) // HEAD_DIM).astype(BF16)
    const = lambda shape: pl.BlockSpec(shape, lambda i, p, q: (0,) * len(shape))
    dv = 2 * HEAD_DIM
    rows_aug = dv + ONES_ROWS
    return pl.pallas_call(
        functools.partial(_diff_kernel, lam_init=lam_init), grid=(b, n_heads, s // tq),
        in_specs=[pl.BlockSpec((1, tq, LANES), lambda i, p, q: (i, q, p)),
                  pl.BlockSpec((1, s, LANES), lambda i, p, q: (i, 0, n_heads + p)),
                  pl.BlockSpec((1, s, dv), lambda i, p, q: (i, 0, 2 * n_heads + p)),
                  pl.BlockSpec((1, 2, tk, tk), lambda i, p, q: (p, 0, 0, 0)),
                  const((1, LANES)), const((1, LANES)), const((LANES, LANES)), const((8, LANES)), const((1, dv))],
        out_specs=pl.BlockSpec((1, tq, dv), lambda i, p, q: (i, q, p)),
        out_shape=jax.ShapeDtypeStruct((b, s, n_heads * dv), BF16),
        scratch_shapes=[pltpu.VMEM((s, LANES), BF16),
                        pltpu.VMEM((nblk, rows_aug, tk), BF16),
                        pltpu.VMEM((LANES, tq), BF16), pltpu.VMEM((LANES, tq), BF16),
                        pltpu.VMEM((rows_aug, tq), F32), pltpu.VMEM((rows_aug, tq), F32),
                        pltpu.VMEM((8, tq), F32)],
        compiler_params=_params(3), name="diff_attn")(proj, proj, proj, bias, gq, gk, bd, lam_rows, subln_g)


def _t5_bucket(dist):
    max_exact = N_BUCKETS // 2
    nf = jnp.maximum(dist, 1).astype(F32)
    large = max_exact + (jnp.log(nf / max_exact) / math.log(MAX_DISTANCE / max_exact)
                         * (N_BUCKETS - max_exact)).astype(jnp.int32)
    large = jnp.minimum(large, N_BUCKETS - 1)
    return jnp.where(dist < max_exact, dist, large)


def _bias_tiles(rel_bias, t):
    assert t >= MAX_DISTANCE
    table = rel_bias.astype(F32)
    krow = lax.broadcasted_iota(jnp.int32, (2, t, t), 1)
    qcol = lax.broadcasted_iota(jnp.int32, (2, t, t), 2)
    dist = qcol - krow + t * lax.broadcasted_iota(jnp.int32, (2, t, t), 0)
    onehot = (_t5_bucket(jnp.maximum(dist, 0))[..., None] == jnp.arange(N_BUCKETS)).astype(F32)
    return jnp.einsum('otkb,bh->hotk', onehot, (table - table[N_BUCKETS - 1]) * LOG2E,
                      precision=lax.Precision.HIGHEST)


def _post_kernel(*refs, n_mix):
    x_ref = refs[0]
    o_refs = refs[1:1 + n_mix]
    wo_ref, g_ref, wg_ref, wu_ref, wd_ref, out_ref, x1_ref, act_ref = refs[1 + n_mix:]
    o = jnp.concatenate([o_ref[0] for o_ref in o_refs], axis=1) if n_mix > 1 else o_refs[0][0]
    d = x_ref.shape[2]
    for lo in range(0, d, PROJ_COL_TILE):
        cols = slice(lo, lo + PROJ_COL_TILE)
        x1_ref[:, cols] = x_ref[0, :, cols] + _dot(o, wo_ref[:, cols])
    h = _rms_rows(x1_ref[...], g_ref[...]).astype(BF16)
    d_ff = wg_ref.shape[1]
    for lo in range(0, d_ff, FFN_COL_TILE):
        cols = slice(lo, min(lo + FFN_COL_TILE, d_ff))
        a = _dot(h, wg_ref[:, cols])
        u = _dot(h, wu_ref[:, cols])
        act_ref[:, cols] = (a * (1.0 / (1.0 + jnp.exp(-a))) * u).astype(BF16)
    for lo in range(0, d, PROJ_COL_TILE):
        cols = slice(lo, lo + PROJ_COL_TILE)
        out_ref[0, :, cols] = x1_ref[:, cols] + _dot(act_ref[...], wd_ref[:, cols])


def _post_call(x, mixes, w_out, g, wg, wu, wd):
    b, s, d = x.shape
    tm = min(TOKEN_TILE, s)
    n_mix = len(mixes)
    tok = lambda width: pl.BlockSpec((1, tm, width), lambda i, j: (i, j, 0))
    res = lambda shape: pl.BlockSpec(shape, lambda i, j: (0, 0), pipeline_mode=pl.Buffered(1))
    return pl.pallas_call(
        functools.partial(_post_kernel, n_mix=n_mix), grid=(b, s // tm),
        in_specs=([tok(d)] + [tok(m.shape[2]) for m in mixes]
                  + [res(w_out.shape), res((1, d)), res(wg.shape), res(wu.shape), res(wd.shape)]),
        out_specs=tok(d),
        out_shape=jax.ShapeDtypeStruct((b, s, d), x.dtype),
        scratch_shapes=[pltpu.VMEM((tm, d), F32), pltpu.VMEM((tm, wg.shape[1]), BF16)],
        compiler_params=_params(2), name="post")(x, *mixes, w_out, g, wg, wu, wd)


def _pair_gain(g):
    return jnp.concatenate([g, g]).astype(F32)[None, :]


def kernel(x, attn_norm_g, ffn_norm_g, even_w_in, fox_forget_b, fox_q_norm_g, fox_k_norm_g, even_w_out, diff_w_in, diff_q_norm_g, diff_k_norm_g, diff_lambda_q1, diff_lambda_k1, diff_lambda_q2, diff_lambda_k2, diff_subln_g, diff_w_out, rel_bias, ffn_w_gate, ffn_w_up, ffn_w_down):
    b, s, d = x.shape
    depth = attn_norm_g.shape[0]
    n_sb = d // (2 * HEAD_DIM)
    n_fox = d // (2 * HEAD_DIM)
    n_diff = d // (2 * HEAD_DIM)
    sb_w = n_sb * HEAD_DIM
    fox_w = n_fox * HEAD_DIM
    main_w = 3 * sb_w + 3 * fox_w
    assert sb_w % LANES == 0 and fox_w % LANES == 0 and n_fox <= LANES
    bias = _bias_tiles(rel_bias, _attn_tiles(s)[1])

    for layer in range(depth):
        g_attn = attn_norm_g[layer][None, :]
        if layer % 2 == 0:
            e = layer // 2
            w_in = even_w_in[e]
            wfg = jnp.pad(w_in[:, main_w:], ((0, 0), (0, LANES - n_fox))).astype(BF16)
            fb = jnp.pad(fox_forget_b[e].astype(F32), (0, LANES - n_fox))[None, :]
            proj, c = _pre_call(x, g_attn, w_in[:, :main_w].astype(BF16), (wfg, fb))
            sbc, fxc = sb_w // LANES, fox_w // LANES
            o_a = _sb_call(proj, sbc, 0, sbc, 2 * sbc)
            o_b = _fox_call(proj, c, _pair_gain(fox_q_norm_g[e]), _pair_gain(fox_k_norm_g[e]),
                            fxc, 3 * sbc, 3 * sbc + fxc, 3 * sbc + 2 * fxc)
            mixes, w_out = [o_a, o_b], even_w_out[e].astype(BF16)
        else:
            o = layer // 2
            proj = _pre_call(x, g_attn, diff_w_in[o].astype(BF16))
            lam_rows = jnp.stack([diff_lambda_q1[o], diff_lambda_k1[o], diff_lambda_q2[o], diff_lambda_k2[o]])
            lam_rows = jnp.pad(lam_rows.astype(F32), ((0, 4), (0, LANES - HEAD_DIM)))
            lam_init = 0.8 - 0.6 * math.exp(-0.3 * layer)
            mix = _diff_call(proj, bias, _pair_gain(diff_q_norm_g[o]), _pair_gain(diff_k_norm_g[o]),
                             lam_rows, diff_subln_g[o].astype(F32)[None, :], n_diff, lam_init)
            mixes, w_out = [mix], diff_w_out[o].astype(BF16)
        x = _post_call(x, mixes, w_out, ffn_norm_g[layer][None, :], ffn_w_gate[layer].astype(BF16),
                       ffn_w_up[layer].astype(BF16), ffn_w_down[layer].astype(BF16))
    return x
```

```python
import functools
import math

import jax
import jax.numpy as jnp
import numpy as np
from jax import lax
from jax.experimental import pallas as pl
from jax.experimental.pallas import tpu as pltpu

F32 = jnp.float32
BF16 = jnp.bfloat16

HEAD_DIM = 64
N_BUCKETS = 32
MAX_DISTANCE = 128
RMS_EPS = 1e-6
LOG2E = 1.4426950408889634
NEG_BIG = -1e30
SIGN_BIT = np.uint32(0x80000000)

LANES = 128
K_TILE = 256
TOKEN_TILE = 512
PROJ_COL_TILE = 512
FFN_COL_TILE = 512
ONES_ROWS = 16
VMEM_LIMIT = 56 * 1024 * 1024


def _dot(a, b):
    return jnp.dot(a, b, preferred_element_type=F32)


def _params(n_axes, vmem=VMEM_LIMIT):
    return pltpu.CompilerParams(dimension_semantics=("arbitrary",) * n_axes, vmem_limit_bytes=vmem)


def _rms_rows(x, g):
    ms = jnp.mean(x * x, axis=-1, keepdims=True)
    return x * lax.rsqrt(ms + RMS_EPS) * g


def _split3(x):
    hi = x.astype(BF16)
    r = x - hi.astype(F32)
    mid = r.astype(BF16)
    lo = (r - mid.astype(F32)).astype(BF16)
    return hi, mid, lo


def _pre_body(x_ref, g_ref, w_ref, o_ref):
    h = _rms_rows(x_ref[0], g_ref[...]).astype(BF16)
    n = w_ref.shape[1]
    for j in range(n // PROJ_COL_TILE):
        cols = slice(j * PROJ_COL_TILE, (j + 1) * PROJ_COL_TILE)
        o_ref[0, :, cols] = _dot(h, w_ref[:, cols]).astype(o_ref.dtype)
    return h


def _pre_odd_kernel(x_ref, g_ref, w_ref, o_ref):
    _pre_body(x_ref, g_ref, w_ref, o_ref)


def _pre_even_kernel(x_ref, g_ref, w_ref, wfg_ref, fb_ref, ltri_ref, o_ref, c_ref, carry_ref):
    h = _pre_body(x_ref, g_ref, w_ref, o_ref)

    @pl.when(pl.program_id(1) == 0)
    def _():
        carry_ref[...] = jnp.zeros_like(carry_ref)

    fg = _dot(h, wfg_ref[...]) + fb_ref[...]
    logf = jnp.minimum(fg, 0.0) - jnp.log1p(jnp.exp(-jnp.abs(fg)))
    tc = ltri_ref.shape[0]
    carry = carry_ref[...]
    for ci in range(logf.shape[0] // tc):
        rows = slice(ci * tc, (ci + 1) * tc)
        hi, mid, lo = _split3(logf[rows])
        cs = _dot(ltri_ref[...], hi) + _dot(ltri_ref[...], mid) + _dot(ltri_ref[...], lo) + carry
        c_ref[0, rows, :] = cs
        carry = cs[tc - 1:tc, :]
    carry_ref[...] = carry


def _pre_call(x, g, w, extra=None):
    b, s, d = x.shape
    n = w.shape[1]
    tm = min(TOKEN_TILE, s)
    grid = (b, s // tm)
    x_spec = pl.BlockSpec((1, tm, d), lambda i, j: (i, j, 0))
    g_spec = pl.BlockSpec((1, d), lambda i, j: (0, 0))
    w_spec = pl.BlockSpec((d, n), lambda i, j: (0, 0))
    o_spec = pl.BlockSpec((1, tm, n), lambda i, j: (i, j, 0))
    o_shape = jax.ShapeDtypeStruct((b, s, n), BF16)
    if extra is None:
        return pl.pallas_call(
            _pre_odd_kernel, grid=grid, in_specs=[x_spec, g_spec, w_spec], out_specs=o_spec,
            out_shape=o_shape, compiler_params=_params(2), name="pre_odd")(x, g, w)
    wfg, fb = extra
    tc = min(K_TILE, tm)
    ltri = (lax.broadcasted_iota(jnp.int32, (tc, tc), 0) >= lax.broadcasted_iota(jnp.int32, (tc, tc), 1)).astype(BF16)
    return pl.pallas_call(
        _pre_even_kernel, grid=grid,
        in_specs=[x_spec, g_spec, w_spec,
                  pl.BlockSpec((d, LANES), lambda i, j: (0, 0)),
                  pl.BlockSpec((1, LANES), lambda i, j: (0, 0)),
                  pl.BlockSpec((tc, tc), lambda i, j: (0, 0))],
        out_specs=[o_spec, pl.BlockSpec((1, tm, LANES), lambda i, j: (i, j, 0))],
        out_shape=[o_shape, jax.ShapeDtypeStruct((b, s, LANES), F32)],
        scratch_shapes=[pltpu.VMEM((1, LANES), F32)],
        compiler_params=_params(2), name="pre_even")(x, g, w, wfg, fb, ltri)


def _key_tile(s):
    tk = min(K_TILE, s)
    assert s % tk == 0 and tk % LANES == 0
    return tk


def _pipeline(nblk, stages):
    carried = [{} for _ in stages]
    for step in range(nblk + len(stages) - 1):
        for i, stage in enumerate(stages):
            d = nblk - 1 - step + i
            if 0 <= d < nblk:
                arg = carried[i - 1].pop(d) if i else None
                carried[i][d] = stage(d, arg)


def _store_split_qt(qn, qat_ref, qbt_ref, col0):
    row = lax.broadcasted_iota(jnp.int32, (LANES, LANES), 0)
    for c in range(qn.shape[0] // LANES):
        cols = slice(col0 + c * LANES, col0 + (c + 1) * LANES)
        qt = qn[c * LANES:(c + 1) * LANES].T
        qat_ref[:, cols] = jnp.where(row < HEAD_DIM, qt, 0.0).astype(BF16)
        qbt_ref[:, cols] = jnp.where(row >= HEAD_DIM, qt, 0.0).astype(BF16)


def _group_rms(x, bd, g):
    ss = _dot((x * x).astype(BF16), bd)
    return x * lax.rsqrt(ss * (1.0 / HEAD_DIM) + RMS_EPS) * g


def _softmax_tile(d, zs, m, acc_refs, vts, tk):
    for cb in range(zs[0].shape[1] // tk):
        a = d + cb
        cols = slice(a * tk, (a + 1) * tk)
        for i, (z, acc_ref, vt) in enumerate(zip(zs, acc_refs, vts)):
            zb = z[:, cb * tk:(cb + 1) * tk]
            m_cur = jnp.max(zb, axis=0, keepdims=True)
            if cb == 0:
                acc_ref[:, cols] = _dot(vt, jnp.exp2(zb - m_cur).astype(BF16))
                m[i][a] = m_cur
            else:
                m_new = jnp.maximum(m[i][a], m_cur)
                p = jnp.exp2(zb - m_new).astype(BF16)
                acc_ref[:, cols] = acc_ref[:, cols] * jnp.exp2(m[i][a] - m_new) + _dot(vt, p)
                m[i][a] = m_new


def _with_ones(vt):
    return jnp.concatenate([vt.astype(BF16), jnp.ones((ONES_ROWS, vt.shape[1]), BF16)], axis=0)


def _sb_kernel(q_ref, k_ref, v_ref, u_ref, o_ref, qat_ref, qbt_ref, acc_ref):
    s = q_ref.shape[1]
    tk = u_ref.shape[0]
    nblk = s // tk
    _store_split_qt(q_ref[0].astype(F32) * (HEAD_DIM ** -0.5 * LOG2E), qat_ref, qbt_ref, 0)
    strict = lax.broadcasted_iota(jnp.int32, (tk, tk), 0) < lax.broadcasted_iota(jnp.int32, (tk, tk), 1)
    r = [[None] * nblk for _ in range(2)]

    def logits(d, _):
        kblk = k_ref[0, d * tk:(d + 1) * tk, :]
        return [_dot(kblk, qt_ref[:, d * tk:]) for qt_ref in (qat_ref, qbt_ref)]

    def suffix_sums(d, zs):
        chains = []
        for cb in range(nblk - d):
            for sub in range(2):
                z = zs[sub][:, cb * tk:(cb + 1) * tk]
                neg_abs = lax.bitcast_convert_type(lax.bitcast_convert_type(z, jnp.uint32) | SIGN_BIT, F32)
                sp = jnp.maximum(z, 0.0) + LOG2E * jnp.log(1.0 + jnp.exp2(neg_abs))
                if cb == 0:
                    sp = jnp.where(strict, sp, 0.0)
                hi = sp.astype(BF16)
                lo = (sp - hi.astype(F32)).astype(BF16)
                cum = _dot(u_ref[...], hi) + _dot(u_ref[...], lo)
                chains.append((cb, sub, z, cum))
        return chains

    def values(d, chains):
        vt = v_ref[0, d * tk:(d + 1) * tk, :].astype(F32).T.astype(BF16)
        for cb, sub, z, cum in chains:
            a = d + cb
            rows = slice(sub * HEAD_DIM, (sub + 1) * HEAD_DIM)
            cols = slice(a * tk, (a + 1) * tk)
            if cb == 0:
                w = jnp.where(strict, jnp.exp2(z - cum), 0.0)
                acc_ref[rows, cols] = _dot(vt[rows], w.astype(BF16))
                r[sub][a] = cum[0:1, :]
            else:
                w = jnp.exp2(z - cum - r[sub][a])
                acc_ref[rows, cols] += _dot(vt[rows], w.astype(BF16))
                r[sub][a] = r[sub][a] + cum[0:1, :]

    _pipeline(nblk, (logits, suffix_sums, values))
    for c in range(s // LANES):
        cols = slice(c * LANES, (c + 1) * LANES)
        o_ref[0, cols, :] = acc_ref[:, cols].T.astype(o_ref.dtype)


def _sb_call(proj, n_pairs, q_col, k_col, v_col):
    b, s, _ = proj.shape
    tk = _key_tile(s)
    u = (lax.broadcasted_iota(jnp.int32, (tk, tk), 1) >= lax.broadcasted_iota(jnp.int32, (tk, tk), 0)).astype(BF16)
    tok = lambda col: pl.BlockSpec((1, s, LANES), lambda i, p: (i, 0, col + p))
    return pl.pallas_call(
        _sb_kernel, grid=(b, n_pairs),
        in_specs=[tok(q_col), tok(k_col), tok(v_col), pl.BlockSpec((tk, tk), lambda i, p: (0, 0))],
        out_specs=tok(0),
        out_shape=jax.ShapeDtypeStruct((b, s, n_pairs * LANES), BF16),
        scratch_shapes=[pltpu.VMEM((LANES, s), BF16), pltpu.VMEM((LANES, s), BF16), pltpu.VMEM((LANES, s), F32)],
        compiler_params=_params(2), name="sb_attn")(proj, proj, proj, u)


def _fox_kernel(q_ref, k_ref, v_ref, c_ref, gq_ref, gk_ref, bd_ref, o_ref, qat_ref, qbt_ref, acca_ref, accb_ref):
    s = q_ref.shape[1]
    tk = _key_tile(s)
    nblk = s // tk
    p_idx = pl.program_id(1)
    for c in range(nblk):
        rows = slice(c * tk, (c + 1) * tk)
        qn = _group_rms(q_ref[0, rows, :].astype(F32), bd_ref[...], gq_ref[...]) * (HEAD_DIM ** -0.5 * LOG2E)
        _store_split_qt(qn, qat_ref, qbt_ref, c * tk)
    sel_r = lax.broadcasted_iota(jnp.int32, (LANES, 2 * LANES), 0)
    sel_l = lax.broadcasted_iota(jnp.int32, (LANES, 2 * LANES), 1)
    sel = (sel_r == 2 * p_idx + (sel_l >= LANES).astype(jnp.int32)).astype(BF16)
    causal = lax.broadcasted_iota(jnp.int32, (tk, tk), 0) <= lax.broadcasted_iota(jnp.int32, (tk, tk), 1)
    m = [[None] * nblk for _ in range(2)]

    def logits(d, _):
        rows = slice(d * tk, (d + 1) * tk)
        kn = _group_rms(k_ref[0, rows, :].astype(F32), bd_ref[...], gk_ref[...]).astype(BF16)
        hi, mid, lo = _split3(c_ref[0, rows, :] * LOG2E)
        cb = _dot(hi, sel) + _dot(mid, sel) + _dot(lo, sel)
        zs = []
        for sub, qt_ref in enumerate((qat_ref, qbt_ref)):
            decay = cb[:, sub * LANES:(sub + 1) * LANES]
            z = _dot(kn, qt_ref[:, d * tk:]) - jnp.concatenate([decay] * ((s - d * tk) // LANES), axis=1)
            tri = jnp.where(causal, z[:, :tk], NEG_BIG)
            zs.append(tri if d == nblk - 1 else jnp.concatenate([tri, z[:, tk:]], axis=1))
        return zs

    def values(d, zs):
        vt = v_ref[0, d * tk:(d + 1) * tk, :].astype(F32).T
        _softmax_tile(d, zs, m, (acca_ref, accb_ref), (_with_ones(vt[:HEAD_DIM]), _with_ones(vt[HEAD_DIM:])), tk)

    _pipeline(nblk, (logits, values))
    for c in range(s // LANES):
        cols = slice(c * LANES, (c + 1) * LANES)
        oa = acca_ref[0:HEAD_DIM, cols] / acca_ref[HEAD_DIM:HEAD_DIM + 1, cols]
        ob = accb_ref[0:HEAD_DIM, cols] / accb_ref[HEAD_DIM:HEAD_DIM + 1, cols]
        o_ref[0, cols, :] = jnp.concatenate([oa, ob], axis=0).T.astype(o_ref.dtype)


def _block_diag_ones():
    blk = lax.broadcasted_iota(jnp.int32, (LANES, LANES), 0) // HEAD_DIM
    return (blk == lax.broadcasted_iota(jnp.int32, (LANES, LANES), 1) // HEAD_DIM).astype(BF16)


def _fox_call(proj, c, gq, gk, n_pairs, q_col, k_col, v_col):
    b, s, _ = proj.shape
    tok = lambda col: pl.BlockSpec((1, s, LANES), lambda i, p: (i, 0, col + p))
    const = lambda shape: pl.BlockSpec(shape, lambda i, p: (0,) * len(shape))
    rows_aug = HEAD_DIM + ONES_ROWS
    return pl.pallas_call(
        _fox_kernel, grid=(b, n_pairs),
        in_specs=[tok(q_col), tok(k_col), tok(v_col), pl.BlockSpec((1, s, LANES), lambda i, p: (i, 0, 0)),
                  const((1, LANES)), const((1, LANES)), const((LANES, LANES))],
        out_specs=tok(0),
        out_shape=jax.ShapeDtypeStruct((b, s, n_pairs * LANES), BF16),
        scratch_shapes=[pltpu.VMEM((LANES, s), BF16), pltpu.VMEM((LANES, s), BF16),
                        pltpu.VMEM((rows_aug, s), F32), pltpu.VMEM((rows_aug, s), F32)],
        compiler_params=_params(2), name="fox_attn")(proj, proj, proj, c, gq, gk, _block_diag_ones())


def _diff_kernel(q_ref, k_ref, v_ref, bias_ref, gq_ref, gk_ref, bd_ref, lam_ref, sg_ref, o_ref,
                 q1t_ref, q2t_ref, acc1_ref, acc2_ref, *, lam_init):
    s = q_ref.shape[1]
    tk = bias_ref.shape[2]
    nblk = s // tk
    dv = v_ref.shape[2]
    for c in range(nblk):
        rows = slice(c * tk, (c + 1) * tk)
        qn = _group_rms(q_ref[0, rows, :].astype(F32), bd_ref[...], gq_ref[...]) * (HEAD_DIM ** -0.5 * LOG2E)
        _store_split_qt(qn, q1t_ref, q2t_ref, c * tk)
    causal = lax.broadcasted_iota(jnp.int32, (tk, tk), 0) <= lax.broadcasted_iota(jnp.int32, (tk, tk), 1)
    m = [[None] * nblk for _ in range(2)]

    def logits(d, _):
        rows = slice(d * tk, (d + 1) * tk)
        kn = _group_rms(k_ref[0, rows, :].astype(F32), bd_ref[...], gk_ref[...]).astype(BF16)
        zs = []
        for qt_ref in (q1t_ref, q2t_ref):
            z = _dot(kn, qt_ref[:, d * tk:])
            parts = [jnp.where(causal, z[:, :tk] + bias_ref[0, 0], NEG_BIG)]
            if d < nblk - 1:
                parts.append(z[:, tk:2 * tk] + bias_ref[0, 1])
            if d < nblk - 2:
                parts.append(z[:, 2 * tk:])
            zs.append(jnp.concatenate(parts, axis=1) if len(parts) > 1 else parts[0])
        return zs

    def values(d, zs):
        vt = _with_ones(v_ref[0, d * tk:(d + 1) * tk, :].astype(F32).T)
        _softmax_tile(d, zs, m, (acc1_ref, acc2_ref), (vt, vt), tk)

    _pipeline(nblk, (logits, values))

    lam = (jnp.exp(jnp.sum(lam_ref[0:1, :] * lam_ref[1:2, :], axis=-1, keepdims=True))
           - jnp.exp(jnp.sum(lam_ref[2:3, :] * lam_ref[3:4, :], axis=-1, keepdims=True)) + lam_init)
    for c in range(s // LANES):
        cols = slice(c * LANES, (c + 1) * LANES)
        o1 = acc1_ref[0:dv, cols] / acc1_ref[dv:dv + 1, cols]
        o2 = acc2_ref[0:dv, cols] / acc2_ref[dv:dv + 1, cols]
        o = o1 - lam * o2
        o = o * lax.rsqrt(jnp.mean(o * o, axis=0, keepdims=True) + RMS_EPS)
        o_ref[0, cols, :] = (o.T * sg_ref[...] * (1.0 - lam_init)).astype(o_ref.dtype)


def _diff_call(proj, bias, gq, gk, lam_rows, subln_g, n_heads, lam_init):
    b, s, _ = proj.shape
    tk = _key_tile(s)
    tok = lambda col: pl.BlockSpec((1, s, LANES), lambda i, p: (i, 0, col + p))
    const = lambda shape: pl.BlockSpec(shape, lambda i, p: (0,) * len(shape))
    dv = 2 * HEAD_DIM
    assert dv == LANES
    rows_aug = dv + ONES_ROWS
    return pl.pallas_call(
        functools.partial(_diff_kernel, lam_init=lam_init), grid=(b, n_heads),
        in_specs=[tok(0), tok(n_heads), tok(2 * n_heads),
                  pl.BlockSpec((1, 2, tk, tk), lambda i, p: (p, 0, 0, 0)),
                  const((1, LANES)), const((1, LANES)), const((LANES, LANES)), const((8, LANES)), const((1, dv))],
        out_specs=tok(0),
        out_shape=jax.ShapeDtypeStruct((b, s, n_heads * dv), BF16),
        scratch_shapes=[pltpu.VMEM((LANES, s), BF16), pltpu.VMEM((LANES, s), BF16),
                        pltpu.VMEM((rows_aug, s), F32), pltpu.VMEM((rows_aug, s), F32)],
        compiler_params=_params(2), name="diff_attn")(proj, proj, proj, bias, gq, gk, _block_diag_ones(),
                                                      lam_rows, subln_g)


def _t5_bucket(dist):
    max_exact = N_BUCKETS // 2
    nf = jnp.maximum(dist, 1).astype(F32)
    large = max_exact + (jnp.log(nf / max_exact) / math.log(MAX_DISTANCE / max_exact)
                         * (N_BUCKETS - max_exact)).astype(jnp.int32)
    large = jnp.minimum(large, N_BUCKETS - 1)
    return jnp.where(dist < max_exact, dist, large)


def _bias_tiles(rel_bias, t):
    assert t >= MAX_DISTANCE
    table = rel_bias.astype(F32)
    krow = lax.broadcasted_iota(jnp.int32, (2, t, t), 1)
    qcol = lax.broadcasted_iota(jnp.int32, (2, t, t), 2)
    dist = qcol - krow + t * lax.broadcasted_iota(jnp.int32, (2, t, t), 0)
    onehot = (_t5_bucket(jnp.maximum(dist, 0))[..., None] == jnp.arange(N_BUCKETS)).astype(F32)
    return jnp.einsum('otkb,bh->hotk', onehot, (table - table[N_BUCKETS - 1]) * LOG2E,
                      precision=lax.Precision.HIGHEST)


def _post_kernel(*refs, n_mix):
    x_ref = refs[0]
    o_refs = refs[1:1 + n_mix]
    wo_ref, g_ref, wg_ref, wu_ref, wd_ref, out_ref, x1_ref, act_ref = refs[1 + n_mix:]
    o = jnp.concatenate([o_ref[0] for o_ref in o_refs], axis=1) if n_mix > 1 else o_refs[0][0]
    d = x_ref.shape[2]
    for lo in range(0, d, PROJ_COL_TILE):
        cols = slice(lo, lo + PROJ_COL_TILE)
        x1_ref[:, cols] = x_ref[0, :, cols] + _dot(o, wo_ref[:, cols])
    h = _rms_rows(x1_ref[...], g_ref[...]).astype(BF16)
    d_ff = wg_ref.shape[1]
    for lo in range(0, d_ff, FFN_COL_TILE):
        cols = slice(lo, min(lo + FFN_COL_TILE, d_ff))
        a = _dot(h, wg_ref[:, cols])
        u = _dot(h, wu_ref[:, cols])
        act_ref[:, cols] = (a * (1.0 / (1.0 + jnp.exp(-a))) * u).astype(BF16)
    for lo in range(0, d, PROJ_COL_TILE):
        cols = slice(lo, lo + PROJ_COL_TILE)
        out_ref[0, :, cols] = x1_ref[:, cols] + _dot(act_ref[...], wd_ref[:, cols])


def _post_call(x, mixes, w_out, g, wg, wu, wd):
    b, s, d = x.shape
    tm = min(TOKEN_TILE, s)
    n_mix = len(mixes)
    tok = lambda width: pl.BlockSpec((1, tm, width), lambda i, j: (i, j, 0))
    res = lambda shape: pl.BlockSpec(shape, lambda i, j: (0, 0), pipeline_mode=pl.Buffered(1))
    return pl.pallas_call(
        functools.partial(_post_kernel, n_mix=n_mix), grid=(b, s // tm),
        in_specs=([tok(d)] + [tok(m.shape[2]) for m in mixes]
                  + [res(w_out.shape), res((1, d)), res(wg.shape), res(wu.shape), res(wd.shape)]),
        out_specs=tok(d),
        out_shape=jax.ShapeDtypeStruct((b, s, d), x.dtype),
        scratch_shapes=[pltpu.VMEM((tm, d), F32), pltpu.VMEM((tm, wg.shape[1]), BF16)],
        compiler_params=_params(2), name="post")(x, *mixes, w_out, g, wg, wu, wd)


def _pair_gain(g):
    return jnp.concatenate([g, g]).astype(F32)[None, :]


def kernel(x, attn_norm_g, ffn_norm_g, even_w_in, fox_forget_b, fox_q_norm_g, fox_k_norm_g, even_w_out, diff_w_in, diff_q_norm_g, diff_k_norm_g, diff_lambda_q1, diff_lambda_k1, diff_lambda_q2, diff_lambda_k2, diff_subln_g, diff_w_out, rel_bias, ffn_w_gate, ffn_w_up, ffn_w_down):
    b, s, d = x.shape
    depth = attn_norm_g.shape[0]
    n_sb = d // (2 * HEAD_DIM)
    n_fox = d // (2 * HEAD_DIM)
    n_diff = d // (2 * HEAD_DIM)
    sb_w = n_sb * HEAD_DIM
    fox_w = n_fox * HEAD_DIM
    main_w = 3 * sb_w + 3 * fox_w
    assert sb_w % LANES == 0 and fox_w % LANES == 0 and n_fox <= LANES
    bias = _bias_tiles(rel_bias, _key_tile(s))

    for layer in range(depth):
        g_attn = attn_norm_g[layer][None, :]
        if layer % 2 == 0:
            e = layer // 2
            w_in = even_w_in[e]
            wfg = jnp.pad(w_in[:, main_w:], ((0, 0), (0, LANES - n_fox))).astype(BF16)
            fb = jnp.pad(fox_forget_b[e].astype(F32), (0, LANES - n_fox))[None, :]
            proj, c = _pre_call(x, g_attn, w_in[:, :main_w].astype(BF16), (wfg, fb))
            sbc, fxc = sb_w // LANES, fox_w // LANES
            o_a = _sb_call(proj, sbc, 0, sbc, 2 * sbc)
            o_b = _fox_call(proj, c, _pair_gain(fox_q_norm_g[e]), _pair_gain(fox_k_norm_g[e]),
                            fxc, 3 * sbc, 3 * sbc + fxc, 3 * sbc + 2 * fxc)
            mixes, w_out = [o_a, o_b], even_w_out[e].astype(BF16)
        else:
            o = layer // 2
            proj = _pre_call(x, g_attn, diff_w_in[o].astype(BF16))
            lam_rows = jnp.stack([diff_lambda_q1[o], diff_lambda_k1[o], diff_lambda_q2[o], diff_lambda_k2[o]])
            lam_rows = jnp.pad(lam_rows.astype(F32), ((0, 4), (0, LANES - HEAD_DIM)))
            lam_init = 0.8 - 0.6 * math.exp(-0.3 * layer)
            mix = _diff_call(proj, bias, _pair_gain(diff_q_norm_g[o]), _pair_gain(diff_k_norm_g[o]),
                             lam_rows, diff_subln_g[o].astype(F32)[None, :], n_diff, lam_init)
            mixes, w_out = [mix], diff_w_out[o].astype(BF16)
        x = _post_call(x, mixes, w_out, ffn_norm_g[layer][None, :], ffn_w_gate[layer].astype(BF16),
                       ffn_w_up[layer].astype(BF16), ffn_w_down[layer].astype(BF16))
    return x
```

```python
import functools
import math

import jax
import jax.numpy as jnp
import numpy as np
from jax import lax
from jax.experimental import pallas as pl
from jax.experimental.pallas import tpu as pltpu

F32 = jnp.float32
BF16 = jnp.bfloat16

HEAD_DIM = 64
N_BUCKETS = 32
MAX_DISTANCE = 128
RMS_EPS = 1e-6
LOG2E = 1.4426950408889634
NEG_BIG = -1e30
SIGN_BIT = np.uint32(0x80000000)

LANES = 128
K_TILE = 256
TOKEN_TILE = 512
PROJ_COL_TILE = 512
FFN_COL_TILE = 512
ONES_ROWS = 16
SB_NEAR_TILES = 2
SB_ZERO_BITS = 152.0
VMEM_LIMIT = 56 * 1024 * 1024


def _dot(a, b):
    return jnp.dot(a, b, preferred_element_type=F32)


def _params(n_axes, vmem=VMEM_LIMIT):
    return pltpu.CompilerParams(dimension_semantics=("arbitrary",) * n_axes, vmem_limit_bytes=vmem)


def _rms_rows(x, g):
    ms = jnp.mean(x * x, axis=-1, keepdims=True)
    return x * lax.rsqrt(ms + RMS_EPS) * g


def _split3(x):
    hi = x.astype(BF16)
    r = x - hi.astype(F32)
    mid = r.astype(BF16)
    lo = (r - mid.astype(F32)).astype(BF16)
    return hi, mid, lo


def _pre_body(x_ref, g_ref, w_ref, o_ref):
    h = _rms_rows(x_ref[0], g_ref[...]).astype(BF16)
    n = w_ref.shape[1]
    for j in range(n // PROJ_COL_TILE):
        cols = slice(j * PROJ_COL_TILE, (j + 1) * PROJ_COL_TILE)
        o_ref[0, :, cols] = _dot(h, w_ref[:, cols]).astype(o_ref.dtype)
    return h


def _pre_odd_kernel(x_ref, g_ref, w_ref, o_ref):
    _pre_body(x_ref, g_ref, w_ref, o_ref)


def _pre_even_kernel(x_ref, g_ref, w_ref, wfg_ref, fb_ref, ltri_ref, o_ref, c_ref, carry_ref):
    h = _pre_body(x_ref, g_ref, w_ref, o_ref)

    @pl.when(pl.program_id(1) == 0)
    def _():
        carry_ref[...] = jnp.zeros_like(carry_ref)

    fg = _dot(h, wfg_ref[...]) + fb_ref[...]
    logf = jnp.minimum(fg, 0.0) - jnp.log1p(jnp.exp(-jnp.abs(fg)))
    tc = ltri_ref.shape[0]
    carry = carry_ref[...]
    for ci in range(logf.shape[0] // tc):
        rows = slice(ci * tc, (ci + 1) * tc)
        hi, mid, lo = _split3(logf[rows])
        cs = _dot(ltri_ref[...], hi) + _dot(ltri_ref[...], mid) + _dot(ltri_ref[...], lo) + carry
        c_ref[0, rows, :] = cs
        carry = cs[tc - 1:tc, :]
    carry_ref[...] = carry


def _pre_call(x, g, w, extra=None):
    b, s, d = x.shape
    n = w.shape[1]
    tm = min(TOKEN_TILE, s)
    grid = (b, s // tm)
    x_spec = pl.BlockSpec((1, tm, d), lambda i, j: (i, j, 0))
    g_spec = pl.BlockSpec((1, d), lambda i, j: (0, 0))
    w_spec = pl.BlockSpec((d, n), lambda i, j: (0, 0))
    o_spec = pl.BlockSpec((1, tm, n), lambda i, j: (i, j, 0))
    o_shape = jax.ShapeDtypeStruct((b, s, n), BF16)
    if extra is None:
        return pl.pallas_call(
            _pre_odd_kernel, grid=grid, in_specs=[x_spec, g_spec, w_spec], out_specs=o_spec,
            out_shape=o_shape, compiler_params=_params(2), name="pre_odd")(x, g, w)
    wfg, fb = extra
    tc = min(K_TILE, tm)
    ltri = (lax.broadcasted_iota(jnp.int32, (tc, tc), 0) >= lax.broadcasted_iota(jnp.int32, (tc, tc), 1)).astype(BF16)
    return pl.pallas_call(
        _pre_even_kernel, grid=grid,
        in_specs=[x_spec, g_spec, w_spec,
                  pl.BlockSpec((d, LANES), lambda i, j: (0, 0)),
                  pl.BlockSpec((1, LANES), lambda i, j: (0, 0)),
                  pl.BlockSpec((tc, tc), lambda i, j: (0, 0))],
        out_specs=[o_spec, pl.BlockSpec((1, tm, LANES), lambda i, j: (i, j, 0))],
        out_shape=[o_shape, jax.ShapeDtypeStruct((b, s, LANES), F32)],
        scratch_shapes=[pltpu.VMEM((1, LANES), F32)],
        compiler_params=_params(2), name="pre_even")(x, g, w, wfg, fb, ltri)


def _key_tile(s):
    tk = min(K_TILE, s)
    assert s % tk == 0 and tk % LANES == 0
    return tk


def _pipeline(nblk, stages):
    carried = [{} for _ in stages]
    for step in range(nblk + len(stages) - 1):
        for i, stage in enumerate(stages):
            d = nblk - 1 - step + i
            if 0 <= d < nblk:
                arg = carried[i - 1].pop(d) if i else None
                carried[i][d] = stage(d, arg)


def _store_split_qt(qn, qat_ref, qbt_ref, col0):
    row = lax.broadcasted_iota(jnp.int32, (LANES, LANES), 0)
    for c in range(qn.shape[0] // LANES):
        cols = slice(col0 + c * LANES, col0 + (c + 1) * LANES)
        qt = qn[c * LANES:(c + 1) * LANES].T
        qat_ref[:, cols] = jnp.where(row < HEAD_DIM, qt, 0.0).astype(BF16)
        qbt_ref[:, cols] = jnp.where(row >= HEAD_DIM, qt, 0.0).astype(BF16)


def _group_rms(x, bd, g):
    ss = _dot((x * x).astype(BF16), bd)
    return x * lax.rsqrt(ss * (1.0 / HEAD_DIM) + RMS_EPS) * g


def _softmax_tile(d, zs, m, acc_refs, vts, tk):
    for cb in range(zs[0].shape[1] // tk):
        a = d + cb
        cols = slice(a * tk, (a + 1) * tk)
        for i, (z, acc_ref, vt) in enumerate(zip(zs, acc_refs, vts)):
            zb = z[:, cb * tk:(cb + 1) * tk]
            m_cur = jnp.max(zb, axis=0, keepdims=True)
            if cb == 0:
                acc_ref[:, cols] = _dot(vt, jnp.exp2(zb - m_cur).astype(BF16))
                m[i][a] = m_cur
            else:
                m_new = jnp.maximum(m[i][a], m_cur)
                p = jnp.exp2(zb - m_new).astype(BF16)
                acc_ref[:, cols] = acc_ref[:, cols] * jnp.exp2(m[i][a] - m_new) + _dot(vt, p)
                m[i][a] = m_new


def _with_ones(vt):
    return jnp.concatenate([vt.astype(BF16), jnp.ones((ONES_ROWS, vt.shape[1]), BF16)], axis=0)


def _sb_kernel(q_ref, k_ref, v_ref, u_ref, o_ref, qat_ref, qbt_ref, acc_ref, r_ref):
    s = q_ref.shape[1]
    tk = u_ref.shape[0]
    nblk = s // tk
    _store_split_qt(q_ref[0].astype(F32) * (HEAD_DIM ** -0.5 * LOG2E), qat_ref, qbt_ref, 0)
    strict = lax.broadcasted_iota(jnp.int32, (tk, tk), 0) < lax.broadcasted_iota(jnp.int32, (tk, tk), 1)
    r = [[None] * nblk for _ in range(2)]

    def suffix_sum(z, tri):
        neg_abs = lax.bitcast_convert_type(lax.bitcast_convert_type(z, jnp.uint32) | SIGN_BIT, F32)
        sp = jnp.maximum(z, 0.0) + LOG2E * jnp.log(1.0 + jnp.exp2(neg_abs))
        if tri:
            sp = jnp.where(strict, sp, 0.0)
        cum = _dot(u_ref[...], sp.astype(BF16))
        return z - sp, cum, cum[0:1, :] + sp[0:1, :]

    def logits(d, _):
        kblk = k_ref[0, d * tk:(d + 1) * tk, :]
        cols = slice(d * tk, min(d + SB_NEAR_TILES, nblk) * tk)
        return [_dot(kblk, qt_ref[:, cols]) for qt_ref in (qat_ref, qbt_ref)]

    def suffix_sums(d, zs):
        chains = []
        for cb in range(zs[0].shape[1] // tk):
            for sub in range(2):
                chains.append((cb, sub) + suffix_sum(zs[sub][:, cb * tk:(cb + 1) * tk], cb == 0))
        return chains

    def values(d, chains):
        vt = v_ref[0, d * tk:(d + 1) * tk, :].astype(F32).T.astype(BF16)
        for cb, sub, log_beta, cum, total in chains:
            a = d + cb
            rows = slice(sub * HEAD_DIM, (sub + 1) * HEAD_DIM)
            cols = slice(a * tk, (a + 1) * tk)
            if cb == 0:
                w = jnp.where(strict, jnp.exp2(log_beta - cum), 0.0)
                acc_ref[rows, cols] = _dot(vt[rows], w.astype(BF16))
                r[sub][a] = total
            else:
                w = jnp.exp2(log_beta - cum - r[sub][a])
                acc_ref[rows, cols] += _dot(vt[rows], w.astype(BF16))
                r[sub][a] = r[sub][a] + total

    _pipeline(nblk, (logits, suffix_sums, values))

    def far_tile(d):
        kblk = k_ref[0, d * tk:(d + 1) * tk, :]
        vt = v_ref[0, d * tk:(d + 1) * tk, :].astype(F32).T.astype(BF16)
        for a in range(d + SB_NEAR_TILES, nblk):
            cols = slice(a * tk, (a + 1) * tk)
            for sub, qt_ref in enumerate((qat_ref, qbt_ref)):
                rows = slice(sub * HEAD_DIM, (sub + 1) * HEAD_DIM)
                log_beta, cum, total = suffix_sum(_dot(kblk, qt_ref[:, cols]), False)
                r_old = r_ref[sub:sub + 1, cols]
                acc_ref[rows, cols] += _dot(vt[rows], jnp.exp2(log_beta - cum - r_old).astype(BF16))
                r_ref[sub:sub + 1, cols] = r_old + total

    if nblk > SB_NEAR_TILES:
        for sub in range(2):
            r_ref[sub:sub + 1, :] = jnp.concatenate(r[sub], axis=1)
        for d in reversed(range(nblk - SB_NEAR_TILES)):
            @pl.when(jnp.min(r_ref[0:2, (d + SB_NEAR_TILES) * tk:]) < SB_ZERO_BITS)
            def _(d=d):
                far_tile(d)

    for c in range(s // LANES):
        cols = slice(c * LANES, (c + 1) * LANES)
        o_ref[0, cols, :] = acc_ref[:, cols].T.astype(o_ref.dtype)


def _sb_call(proj, n_pairs, q_col, k_col, v_col):
    b, s, _ = proj.shape
    tk = _key_tile(s)
    u = (lax.broadcasted_iota(jnp.int32, (tk, tk), 1) > lax.broadcasted_iota(jnp.int32, (tk, tk), 0)).astype(BF16)
    tok = lambda col: pl.BlockSpec((1, s, LANES), lambda i, p: (i, 0, col + p))
    return pl.pallas_call(
        _sb_kernel, grid=(b, n_pairs),
        in_specs=[tok(q_col), tok(k_col), tok(v_col), pl.BlockSpec((tk, tk), lambda i, p: (0, 0))],
        out_specs=tok(0),
        out_shape=jax.ShapeDtypeStruct((b, s, n_pairs * LANES), BF16),
        scratch_shapes=[pltpu.VMEM((LANES, s), BF16), pltpu.VMEM((LANES, s), BF16), pltpu.VMEM((LANES, s), F32),
                        pltpu.VMEM((8, s), F32)],
        compiler_params=_params(2), name="sb_attn")(proj, proj, proj, u)


def _fox_kernel(q_ref, k_ref, v_ref, c_ref, gq_ref, gk_ref, bd_ref, o_ref, qat_ref, qbt_ref, acca_ref, accb_ref):
    s = q_ref.shape[1]
    tk = _key_tile(s)
    nblk = s // tk
    p_idx = pl.program_id(1)
    for c in range(nblk):
        rows = slice(c * tk, (c + 1) * tk)
        qn = _group_rms(q_ref[0, rows, :].astype(F32), bd_ref[...], gq_ref[...]) * (HEAD_DIM ** -0.5 * LOG2E)
        _store_split_qt(qn, qat_ref, qbt_ref, c * tk)
    sel_r = lax.broadcasted_iota(jnp.int32, (LANES, 2 * LANES), 0)
    sel_l = lax.broadcasted_iota(jnp.int32, (LANES, 2 * LANES), 1)
    sel = (sel_r == 2 * p_idx + (sel_l >= LANES).astype(jnp.int32)).astype(BF16)
    causal = lax.broadcasted_iota(jnp.int32, (tk, tk), 0) <= lax.broadcasted_iota(jnp.int32, (tk, tk), 1)
    m = [[None] * nblk for _ in range(2)]

    def logits(d, _):
        rows = slice(d * tk, (d + 1) * tk)
        kn = _group_rms(k_ref[0, rows, :].astype(F32), bd_ref[...], gk_ref[...]).astype(BF16)
        hi, mid, lo = _split3(c_ref[0, rows, :] * LOG2E)
        cb = _dot(hi, sel) + _dot(mid, sel) + _dot(lo, sel)
        zs = []
        for sub, qt_ref in enumerate((qat_ref, qbt_ref)):
            decay = cb[:, sub * LANES:(sub + 1) * LANES]
            z = _dot(kn, qt_ref[:, d * tk:]) - jnp.concatenate([decay] * ((s - d * tk) // LANES), axis=1)
            tri = jnp.where(causal, z[:, :tk], NEG_BIG)
            zs.append(tri if d == nblk - 1 else jnp.concatenate([tri, z[:, tk:]], axis=1))
        return zs

    def values(d, zs):
        vt = v_ref[0, d * tk:(d + 1) * tk, :].astype(F32).T
        _softmax_tile(d, zs, m, (acca_ref, accb_ref), (_with_ones(vt[:HEAD_DIM]), _with_ones(vt[HEAD_DIM:])), tk)

    _pipeline(nblk, (logits, values))
    for c in range(s // LANES):
        cols = slice(c * LANES, (c + 1) * LANES)
        oa = acca_ref[0:HEAD_DIM, cols] / acca_ref[HEAD_DIM:HEAD_DIM + 1, cols]
        ob = accb_ref[0:HEAD_DIM, cols] / accb_ref[HEAD_DIM:HEAD_DIM + 1, cols]
        o_ref[0, cols, :] = jnp.concatenate([oa, ob], axis=0).T.astype(o_ref.dtype)


def _block_diag_ones():
    blk = lax.broadcasted_iota(jnp.int32, (LANES, LANES), 0) // HEAD_DIM
    return (blk == lax.broadcasted_iota(jnp.int32, (LANES, LANES), 1) // HEAD_DIM).astype(BF16)


def _fox_call(proj, c, gq, gk, n_pairs, q_col, k_col, v_col):
    b, s, _ = proj.shape
    tok = lambda col: pl.BlockSpec((1, s, LANES), lambda i, p: (i, 0, col + p))
    const = lambda shape: pl.BlockSpec(shape, lambda i, p: (0,) * len(shape))
    rows_aug = HEAD_DIM + ONES_ROWS
    return pl.pallas_call(
        _fox_kernel, grid=(b, n_pairs),
        in_specs=[tok(q_col), tok(k_col), tok(v_col), pl.BlockSpec((1, s, LANES), lambda i, p: (i, 0, 0)),
                  const((1, LANES)), const((1, LANES)), const((LANES, LANES))],
        out_specs=tok(0),
        out_shape=jax.ShapeDtypeStruct((b, s, n_pairs * LANES), BF16),
        scratch_shapes=[pltpu.VMEM((LANES, s), BF16), pltpu.VMEM((LANES, s), BF16),
                        pltpu.VMEM((rows_aug, s), F32), pltpu.VMEM((rows_aug, s), F32)],
        compiler_params=_params(2), name="fox_attn")(proj, proj, proj, c, gq, gk, _block_diag_ones())


def _diff_kernel(q_ref, k_ref, v_ref, bias_ref, gq_ref, gk_ref, bd_ref, lam_ref, sg_ref, o_ref,
                 q1t_ref, q2t_ref, acc1_ref, acc2_ref, *, lam_init):
    s = q_ref.shape[1]
    tk = bias_ref.shape[2]
    nblk = s // tk
    dv = v_ref.shape[2]
    for c in range(nblk):
        rows = slice(c * tk, (c + 1) * tk)
        qn = _group_rms(q_ref[0, rows, :].astype(F32), bd_ref[...], gq_ref[...]) * (HEAD_DIM ** -0.5 * LOG2E)
        _store_split_qt(qn, q1t_ref, q2t_ref, c * tk)
    causal = lax.broadcasted_iota(jnp.int32, (tk, tk), 0) <= lax.broadcasted_iota(jnp.int32, (tk, tk), 1)
    m = [[None] * nblk for _ in range(2)]

    def logits(d, _):
        rows = slice(d * tk, (d + 1) * tk)
        kn = _group_rms(k_ref[0, rows, :].astype(F32), bd_ref[...], gk_ref[...]).astype(BF16)
        zs = []
        for qt_ref in (q1t_ref, q2t_ref):
            z = _dot(kn, qt_ref[:, d * tk:])
            parts = [jnp.where(causal, z[:, :tk] + bias_ref[0, 0], NEG_BIG)]
            if d < nblk - 1:
                parts.append(z[:, tk:2 * tk] + bias_ref[0, 1])
            if d < nblk - 2:
                parts.append(z[:, 2 * tk:])
            zs.append(jnp.concatenate(parts, axis=1) if len(parts) > 1 else parts[0])
        return zs

    def values(d, zs):
        vt = _with_ones(v_ref[0, d * tk:(d + 1) * tk, :].astype(F32).T)
        _softmax_tile(d, zs, m, (acc1_ref, acc2_ref), (vt, vt), tk)

    _pipeline(nblk, (logits, values))

    lam = (jnp.exp(jnp.sum(lam_ref[0:1, :] * lam_ref[1:2, :], axis=-1, keepdims=True))
           - jnp.exp(jnp.sum(lam_ref[2:3, :] * lam_ref[3:4, :], axis=-1, keepdims=True)) + lam_init)
    for c in range(s // LANES):
        cols = slice(c * LANES, (c + 1) * LANES)
        o1 = acc1_ref[0:dv, cols] / acc1_ref[dv:dv + 1, cols]
        o2 = acc2_ref[0:dv, cols] / acc2_ref[dv:dv + 1, cols]
        o = o1 - lam * o2
        o = o * lax.rsqrt(jnp.mean(o * o, axis=0, keepdims=True) + RMS_EPS)
        o_ref[0, cols, :] = (o.T * sg_ref[...] * (1.0 - lam_init)).astype(o_ref.dtype)


def _diff_call(proj, bias, gq, gk, lam_rows, subln_g, n_heads, lam_init):
    b, s, _ = proj.shape
    tk = _key_tile(s)
    tok = lambda col: pl.BlockSpec((1, s, LANES), lambda i, p: (i, 0, col + p))
    const = lambda shape: pl.BlockSpec(shape, lambda i, p: (0,) * len(shape))
    dv = 2 * HEAD_DIM
    assert dv == LANES
    rows_aug = dv + ONES_ROWS
    return pl.pallas_call(
        functools.partial(_diff_kernel, lam_init=lam_init), grid=(b, n_heads),
        in_specs=[tok(0), tok(n_heads), tok(2 * n_heads),
                  pl.BlockSpec((1, 2, tk, tk), lambda i, p: (p, 0, 0, 0)),
                  const((1, LANES)), const((1, LANES)), const((LANES, LANES)), const((8, LANES)), const((1, dv))],
        out_specs=tok(0),
        out_shape=jax.ShapeDtypeStruct((b, s, n_heads * dv), BF16),
        scratch_shapes=[pltpu.VMEM((LANES, s), BF16), pltpu.VMEM((LANES, s), BF16),
                        pltpu.VMEM((rows_aug, s), F32), pltpu.VMEM((rows_aug, s), F32)],
        compiler_params=_params(2), name="diff_attn")(proj, proj, proj, bias, gq, gk, _block_diag_ones(),
                                                      lam_rows, subln_g)


def _t5_bucket(dist):
    max_exact = N_BUCKETS // 2
    nf = jnp.maximum(dist, 1).astype(F32)
    large = max_exact + (jnp.log(nf / max_exact) / math.log(MAX_DISTANCE / max_exact)
                         * (N_BUCKETS - max_exact)).astype(jnp.int32)
    large = jnp.minimum(large, N_BUCKETS - 1)
    return jnp.where(dist < max_exact, dist, large)


def _bias_tiles(rel_bias, t):
    assert t >= MAX_DISTANCE
    table = rel_bias.astype(F32)
    krow = lax.broadcasted_iota(jnp.int32, (2, t, t), 1)
    qcol = lax.broadcasted_iota(jnp.int32, (2, t, t), 2)
    dist = qcol - krow + t * lax.broadcasted_iota(jnp.int32, (2, t, t), 0)
    onehot = (_t5_bucket(jnp.maximum(dist, 0))[..., None] == jnp.arange(N_BUCKETS)).astype(F32)
    return jnp.einsum('otkb,bh->hotk', onehot, (table - table[N_BUCKETS - 1]) * LOG2E,
                      precision=lax.Precision.HIGHEST)


def _post_kernel(*refs, n_mix):
    x_ref = refs[0]
    o_refs = refs[1:1 + n_mix]
    wo_ref, g_ref, wg_ref, wu_ref, wd_ref, out_ref, x1_ref, act_ref = refs[1 + n_mix:]
    o = jnp.concatenate([o_ref[0] for o_ref in o_refs], axis=1) if n_mix > 1 else o_refs[0][0]
    d = x_ref.shape[2]
    for lo in range(0, d, PROJ_COL_TILE):
        cols = slice(lo, lo + PROJ_COL_TILE)
        x1_ref[:, cols] = x_ref[0, :, cols] + _dot(o, wo_ref[:, cols])
    h = _rms_rows(x1_ref[...], g_ref[...]).astype(BF16)
    d_ff = wg_ref.shape[1]
    for lo in range(0, d_ff, FFN_COL_TILE):
        cols = slice(lo, min(lo + FFN_COL_TILE, d_ff))
        a = _dot(h, wg_ref[:, cols])
        u = _dot(h, wu_ref[:, cols])
        act_ref[:, cols] = (a * (1.0 / (1.0 + jnp.exp(-a))) * u).astype(BF16)
    for lo in range(0, d, PROJ_COL_TILE):
        cols = slice(lo, lo + PROJ_COL_TILE)
        out_ref[0, :, cols] = x1_ref[:, cols] + _dot(act_ref[...], wd_ref[:, cols])


def _post_call(x, mixes, w_out, g, wg, wu, wd):
    b, s, d = x.shape
    tm = min(TOKEN_TILE, s)
    n_mix = len(mixes)
    tok = lambda width: pl.BlockSpec((1, tm, width), lambda i, j: (i, j, 0))
    res = lambda shape: pl.BlockSpec(shape, lambda i, j: (0, 0), pipeline_mode=pl.Buffered(1))
    return pl.pallas_call(
        functools.partial(_post_kernel, n_mix=n_mix), grid=(b, s // tm),
        in_specs=([tok(d)] + [tok(m.shape[2]) for m in mixes]
                  + [res(w_out.shape), res((1, d)), res(wg.shape), res(wu.shape), res(wd.shape)]),
        out_specs=tok(d),
        out_shape=jax.ShapeDtypeStruct((b, s, d), x.dtype),
        scratch_shapes=[pltpu.VMEM((tm, d), F32), pltpu.VMEM((tm, wg.shape[1]), BF16)],
        compiler_params=_params(2), name="post")(x, *mixes, w_out, g, wg, wu, wd)


def _pair_gain(g):
    return jnp.concatenate([g, g]).astype(F32)[None, :]


def kernel(x, attn_norm_g, ffn_norm_g, even_w_in, fox_forget_b, fox_q_norm_g, fox_k_norm_g, even_w_out, diff_w_in, diff_q_norm_g, diff_k_norm_g, diff_lambda_q1, diff_lambda_k1, diff_lambda_q2, diff_lambda_k2, diff_subln_g, diff_w_out, rel_bias, ffn_w_gate, ffn_w_up, ffn_w_down):
    b, s, d = x.shape
    depth = attn_norm_g.shape[0]
    n_sb = d // (2 * HEAD_DIM)
    n_fox = d // (2 * HEAD_DIM)
    n_diff = d // (2 * HEAD_DIM)
    sb_w = n_sb * HEAD_DIM
    fox_w = n_fox * HEAD_DIM
    main_w = 3 * sb_w + 3 * fox_w
    assert sb_w % LANES == 0 and fox_w % LANES == 0 and n_fox <= LANES
    bias = _bias_tiles(rel_bias, _key_tile(s))

    for layer in range(depth):
        g_attn = attn_norm_g[layer][None, :]
        if layer % 2 == 0:
            e = layer // 2
            w_in = even_w_in[e]
            wfg = jnp.pad(w_in[:, main_w:], ((0, 0), (0, LANES - n_fox))).astype(BF16)
            fb = jnp.pad(fox_forget_b[e].astype(F32), (0, LANES - n_fox))[None, :]
            proj, c = _pre_call(x, g_attn, w_in[:, :main_w].astype(BF16), (wfg, fb))
            sbc, fxc = sb_w // LANES, fox_w // LANES
            o_a = _sb_call(proj, sbc, 0, sbc, 2 * sbc)
            o_b = _fox_call(proj, c, _pair_gain(fox_q_norm_g[e]), _pair_gain(fox_k_norm_g[e]),
                            fxc, 3 * sbc, 3 * sbc + fxc, 3 * sbc + 2 * fxc)
            mixes, w_out = [o_a, o_b], even_w_out[e].astype(BF16)
        else:
            o = layer // 2
            proj = _pre_call(x, g_attn, diff_w_in[o].astype(BF16))
            lam_rows = jnp.stack([diff_lambda_q1[o], diff_lambda_k1[o], diff_lambda_q2[o], diff_lambda_k2[o]])
            lam_rows = jnp.pad(lam_rows.astype(F32), ((0, 4), (0, LANES - HEAD_DIM)))
            lam_init = 0.8 - 0.6 * math.exp(-0.3 * layer)
            mix = _diff_call(proj, bias, _pair_gain(diff_q_norm_g[o]), _pair_gain(diff_k_norm_g[o]),
                             lam_rows, diff_subln_g[o].astype(F32)[None, :], n_diff, lam_init)
            mixes, w_out = [mix], diff_w_out[o].astype(BF16)
        x = _post_call(x, mixes, w_out, ffn_norm_g[layer][None, :], ffn_w_gate[layer].astype(BF16),
                       ffn_w_up[layer].astype(BF16), ffn_w_down[layer].astype(BF16))
    return x
```

```python
import functools
import math

import jax
import jax.numpy as jnp
import numpy as np
from jax import lax
from jax.experimental import pallas as pl
from jax.experimental.pallas import tpu as pltpu

F32 = jnp.float32
BF16 = jnp.bfloat16

HEAD_DIM = 64
N_BUCKETS = 32
MAX_DISTANCE = 128
RMS_EPS = 1e-6
LOG2E = 1.4426950408889634
NEG_BIG = -1e30
SIGN_BIT = np.uint32(0x80000000)

LANES = 128
K_TILE = 256
TOKEN_TILE = 512
PROJ_COL_TILE = 512
FFN_COL_TILE = 512
ONES_ROWS = 16
SB_NEAR_TILES = 2
SB_ZERO_BITS = 152.0
SAFE_LOGIT_BITS = 30.0
QK_BOUND_SLACK = 1.01
VMEM_LIMIT = 56 * 1024 * 1024


def _dot(a, b):
    return jnp.dot(a, b, preferred_element_type=F32)


def _params(n_axes, vmem=VMEM_LIMIT):
    return pltpu.CompilerParams(dimension_semantics=("arbitrary",) * n_axes, vmem_limit_bytes=vmem)


def _rms_rows(x, g):
    ms = jnp.mean(x * x, axis=-1, keepdims=True)
    return x * lax.rsqrt(ms + RMS_EPS) * g


def _split3(x):
    hi = x.astype(BF16)
    r = x - hi.astype(F32)
    mid = r.astype(BF16)
    lo = (r - mid.astype(F32)).astype(BF16)
    return hi, mid, lo


def _pre_body(x_ref, g_ref, w_ref, o_ref):
    h = _rms_rows(x_ref[0], g_ref[...]).astype(BF16)
    n = w_ref.shape[1]
    for j in range(n // PROJ_COL_TILE):
        cols = slice(j * PROJ_COL_TILE, (j + 1) * PROJ_COL_TILE)
        o_ref[0, :, cols] = _dot(h, w_ref[:, cols]).astype(o_ref.dtype)
    return h


def _pre_odd_kernel(x_ref, g_ref, w_ref, o_ref):
    _pre_body(x_ref, g_ref, w_ref, o_ref)


def _pre_even_kernel(x_ref, g_ref, w_ref, wfg_ref, fb_ref, ltri_ref, o_ref, c_ref, carry_ref):
    h = _pre_body(x_ref, g_ref, w_ref, o_ref)

    @pl.when(pl.program_id(1) == 0)
    def _():
        carry_ref[...] = jnp.zeros_like(carry_ref)

    fg = _dot(h, wfg_ref[...]) + fb_ref[...]
    logf = jnp.minimum(fg, 0.0) - jnp.log1p(jnp.exp(-jnp.abs(fg)))
    tc = ltri_ref.shape[0]
    carry = carry_ref[...]
    for ci in range(logf.shape[0] // tc):
        rows = slice(ci * tc, (ci + 1) * tc)
        hi, mid, lo = _split3(logf[rows])
        cs = _dot(ltri_ref[...], hi) + _dot(ltri_ref[...], mid) + _dot(ltri_ref[...], lo) + carry
        c_ref[0, rows, :] = cs
        carry = cs[tc - 1:tc, :]
    carry_ref[...] = carry


def _pre_call(x, g, w, extra=None):
    b, s, d = x.shape
    n = w.shape[1]
    tm = min(TOKEN_TILE, s)
    grid = (b, s // tm)
    x_spec = pl.BlockSpec((1, tm, d), lambda i, j: (i, j, 0))
    g_spec = pl.BlockSpec((1, d), lambda i, j: (0, 0))
    w_spec = pl.BlockSpec((d, n), lambda i, j: (0, 0))
    o_spec = pl.BlockSpec((1, tm, n), lambda i, j: (i, j, 0))
    o_shape = jax.ShapeDtypeStruct((b, s, n), BF16)
    if extra is None:
        return pl.pallas_call(
            _pre_odd_kernel, grid=grid, in_specs=[x_spec, g_spec, w_spec], out_specs=o_spec,
            out_shape=o_shape, compiler_params=_params(2), name="pre_odd")(x, g, w)
    wfg, fb = extra
    tc = min(K_TILE, tm)
    ltri = (lax.broadcasted_iota(jnp.int32, (tc, tc), 0) >= lax.broadcasted_iota(jnp.int32, (tc, tc), 1)).astype(BF16)
    return pl.pallas_call(
        _pre_even_kernel, grid=grid,
        in_specs=[x_spec, g_spec, w_spec,
                  pl.BlockSpec((d, LANES), lambda i, j: (0, 0)),
                  pl.BlockSpec((1, LANES), lambda i, j: (0, 0)),
                  pl.BlockSpec((tc, tc), lambda i, j: (0, 0))],
        out_specs=[o_spec, pl.BlockSpec((1, tm, LANES), lambda i, j: (i, j, 0))],
        out_shape=[o_shape, jax.ShapeDtypeStruct((b, s, LANES), F32)],
        scratch_shapes=[pltpu.VMEM((1, LANES), F32)],
        compiler_params=_params(2), name="pre_even")(x, g, w, wfg, fb, ltri)


def _key_tile(s):
    tk = min(K_TILE, s)
    assert s % tk == 0 and tk % LANES == 0
    return tk


def _pipeline(items, stages):
    carried = [{} for _ in stages]
    for step in range(len(items) + len(stages) - 1):
        for i, stage in enumerate(stages):
            idx = step - i
            if 0 <= idx < len(items):
                arg = carried[i - 1].pop(idx) if i else None
                carried[i][idx] = stage(items[idx], arg)


def _descending(n):
    return list(reversed(range(n)))


def _store_split_qt(qn, qat_ref, qbt_ref, col0):
    row = lax.broadcasted_iota(jnp.int32, (LANES, LANES), 0)
    for c in range(qn.shape[0] // LANES):
        cols = slice(col0 + c * LANES, col0 + (c + 1) * LANES)
        qt = qn[c * LANES:(c + 1) * LANES].T
        qat_ref[:, cols] = jnp.where(row < HEAD_DIM, qt, 0.0).astype(BF16)
        qbt_ref[:, cols] = jnp.where(row >= HEAD_DIM, qt, 0.0).astype(BF16)


def _group_rms(x, bd, g):
    ss = _dot((x * x).astype(BF16), bd)
    return x * lax.rsqrt(ss * (1.0 / HEAD_DIM) + RMS_EPS) * g


def _softmax_probs(d, zs, m, tk):
    out = []
    for cb in range(zs[0].shape[1] // tk):
        a = d + cb
        for i, z in enumerate(zs):
            zb = z[:, cb * tk:(cb + 1) * tk]
            m_cur = jnp.max(zb, axis=0, keepdims=True)
            if a == d:
                out.append((a, i, jnp.exp2(zb - m_cur).astype(BF16), None))
                m[i][a] = m_cur
            else:
                m_new = jnp.maximum(m[i][a], m_cur)
                out.append((a, i, jnp.exp2(zb - m_new).astype(BF16), jnp.exp2(m[i][a] - m_new)))
                m[i][a] = m_new
    return out


def _accumulate(probs, acc_refs, vts, tk):
    for a, i, p, alpha in probs:
        cols = slice(a * tk, (a + 1) * tk)
        if alpha is None:
            acc_refs[i][:, cols] = _dot(vts[i], p)
        else:
            acc_refs[i][:, cols] = acc_refs[i][:, cols] * alpha + _dot(vts[i], p)


def _with_ones(vt):
    return jnp.concatenate([vt.astype(BF16), jnp.ones((ONES_ROWS, vt.shape[1]), BF16)], axis=0)


def _sb_kernel(q_ref, k_ref, v_ref, u_ref, o_ref, qat_ref, qbt_ref, acc_ref, r_ref):
    s = q_ref.shape[1]
    tk = u_ref.shape[0]
    nblk = s // tk
    strict = lax.broadcasted_iota(jnp.int32, (tk, tk), 0) < lax.broadcasted_iota(jnp.int32, (tk, tk), 1)
    r = [[None] * nblk for _ in range(2)]

    def suffix_sum(z, tri):
        neg_abs = lax.bitcast_convert_type(lax.bitcast_convert_type(z, jnp.uint32) | SIGN_BIT, F32)
        sp = jnp.maximum(z, 0.0) + LOG2E * jnp.log(1.0 + jnp.exp2(neg_abs))
        if tri:
            sp = jnp.where(strict, sp, 0.0)
        cum = _dot(u_ref[...], sp.astype(BF16))
        return z - sp, cum, cum[0:1, :] + sp[0:1, :]

    def queries(d, _):
        qs = q_ref[0, d * tk:(d + 1) * tk, :].astype(F32) * (HEAD_DIM ** -0.5 * LOG2E)
        _store_split_qt(qs, qat_ref, qbt_ref, d * tk)

    def logits(d, _):
        kblk = k_ref[0, d * tk:(d + 1) * tk, :]
        cols = slice(d * tk, min(d + SB_NEAR_TILES, nblk) * tk)
        return [_dot(kblk, qt_ref[:, cols]) for qt_ref in (qat_ref, qbt_ref)]

    def suffix_sums(d, zs):
        chains = []
        for cb in range(zs[0].shape[1] // tk):
            for sub in range(2):
                chains.append((cb, sub) + suffix_sum(zs[sub][:, cb * tk:(cb + 1) * tk], cb == 0))
        return chains

    def values(d, chains):
        vt = v_ref[0, d * tk:(d + 1) * tk, :].astype(F32).T.astype(BF16)
        for cb, sub, log_beta, cum, total in chains:
            a = d + cb
            rows = slice(sub * HEAD_DIM, (sub + 1) * HEAD_DIM)
            cols = slice(a * tk, (a + 1) * tk)
            if cb == 0:
                w = jnp.where(strict, jnp.exp2(log_beta - cum), 0.0)
                acc_ref[rows, cols] = _dot(vt[rows], w.astype(BF16))
                r[sub][a] = total
            else:
                w = jnp.exp2(log_beta - cum - r[sub][a])
                acc_ref[rows, cols] += _dot(vt[rows], w.astype(BF16))
                r[sub][a] = r[sub][a] + total

    _pipeline(_descending(nblk), (queries, logits, suffix_sums, values))

    def far_tile(d):
        kblk = k_ref[0, d * tk:(d + 1) * tk, :]
        vt = v_ref[0, d * tk:(d + 1) * tk, :].astype(F32).T.astype(BF16)
        for a in range(d + SB_NEAR_TILES, nblk):
            cols = slice(a * tk, (a + 1) * tk)
            for sub, qt_ref in enumerate((qat_ref, qbt_ref)):
                rows = slice(sub * HEAD_DIM, (sub + 1) * HEAD_DIM)
                log_beta, cum, total = suffix_sum(_dot(kblk, qt_ref[:, cols]), False)
                r_old = r_ref[sub:sub + 1, cols]
                acc_ref[rows, cols] += _dot(vt[rows], jnp.exp2(log_beta - cum - r_old).astype(BF16))
                r_ref[sub:sub + 1, cols] = r_old + total

    if nblk > SB_NEAR_TILES:
        for sub in range(2):
            r_ref[sub:sub + 1, :] = jnp.concatenate(r[sub], axis=1)
        for d in reversed(range(nblk - SB_NEAR_TILES)):
            @pl.when(jnp.min(r_ref[0:2, (d + SB_NEAR_TILES) * tk:]) < SB_ZERO_BITS)
            def _(d=d):
                far_tile(d)

    for c in range(s // LANES):
        cols = slice(c * LANES, (c + 1) * LANES)
        o_ref[0, cols, :] = acc_ref[:, cols].T.astype(o_ref.dtype)


def _sb_call(proj, n_pairs, q_col, k_col, v_col):
    b, s, _ = proj.shape
    tk = _key_tile(s)
    u = (lax.broadcasted_iota(jnp.int32, (tk, tk), 1) > lax.broadcasted_iota(jnp.int32, (tk, tk), 0)).astype(BF16)
    tok = lambda col: pl.BlockSpec((1, s, LANES), lambda i, p: (i, 0, col + p))
    return pl.pallas_call(
        _sb_kernel, grid=(b, n_pairs),
        in_specs=[tok(q_col), tok(k_col), tok(v_col), pl.BlockSpec((tk, tk), lambda i, p: (0, 0))],
        out_specs=tok(0),
        out_shape=jax.ShapeDtypeStruct((b, s, n_pairs * LANES), BF16),
        scratch_shapes=[pltpu.VMEM((LANES, s), BF16), pltpu.VMEM((LANES, s), BF16), pltpu.VMEM((LANES, s), F32),
                        pltpu.VMEM((8, s), F32)],
        compiler_params=_params(2), name="sb_attn")(proj, proj, proj, u)


def _qk_logit_bound(gq_ref, gk_ref):
    gains = jnp.max(jnp.abs(gq_ref[...])) * jnp.max(jnp.abs(gk_ref[...]))
    return gains * (HEAD_DIM ** 0.5 * LOG2E * QK_BOUND_SLACK)


def _fox_kernel(q_ref, k_ref, v_ref, c_ref, gq_ref, gk_ref, bd_ref, o_ref,
                qat_ref, qbt_ref, acca_ref, accb_ref, ct_ref, kn_ref, vta_ref, vtb_ref, cb_ref):
    s = q_ref.shape[1]
    tk = _key_tile(s)
    nblk = s // tk
    p_idx = pl.program_id(1)
    sel_r = lax.broadcasted_iota(jnp.int32, (LANES, 2 * LANES), 0)
    sel_l = lax.broadcasted_iota(jnp.int32, (LANES, 2 * LANES), 1)
    sel = (sel_r == 2 * p_idx + (sel_l >= LANES).astype(jnp.int32)).astype(BF16)
    causal = lax.broadcasted_iota(jnp.int32, (tk, tk), 0) <= lax.broadcasted_iota(jnp.int32, (tk, tk), 1)
    head_row = lax.broadcasted_iota(jnp.int32, (LANES, LANES), 0)
    qt_refs, acc_refs = (qat_ref, qbt_ref), (acca_ref, accb_ref)

    def prepare_queries(a):
        qn = _group_rms(q_ref[0, a * tk:(a + 1) * tk, :].astype(F32), bd_ref[...], gq_ref[...])
        _store_split_qt(qn * (HEAD_DIM ** -0.5 * LOG2E), qat_ref, qbt_ref, a * tk)

    def normed_keys(d):
        return _group_rms(k_ref[0, d * tk:(d + 1) * tk, :].astype(F32), bd_ref[...], gk_ref[...]).astype(BF16)

    def key_decay(d):
        hi, mid, lo = _split3(c_ref[0, d * tk:(d + 1) * tk, :] * LOG2E)
        return _dot(hi, sel) + _dot(mid, sel) + _dot(lo, sel)

    def values_t(d):
        vt = v_ref[0, d * tk:(d + 1) * tk, :].astype(F32).T
        return _with_ones(vt[:HEAD_DIM]), _with_ones(vt[HEAD_DIM:])

    def finalize(a):
        for c in range(a * tk // LANES, (a + 1) * tk // LANES):
            cols = slice(c * LANES, (c + 1) * LANES)
            oa = acca_ref[0:HEAD_DIM, cols] / acca_ref[HEAD_DIM:HEAD_DIM + 1, cols]
            ob = accb_ref[0:HEAD_DIM, cols] / accb_ref[HEAD_DIM:HEAD_DIM + 1, cols]
            o_ref[0, cols, :] = jnp.concatenate([oa, ob], axis=0).T.astype(o_ref.dtype)

    def bounded_sweep():
        def prepare(item, _):
            a, d = item
            if d == a:
                prepare_queries(a)
                for c in range(a * tk // LANES, (a + 1) * tk // LANES):
                    cols = slice(c * LANES, (c + 1) * LANES)
                    ct = (c_ref[0, cols, :] * LOG2E).T
                    for sub in range(2):
                        ct_ref[sub:sub + 1, cols] = jnp.sum(jnp.where(head_row == 2 * p_idx + sub, ct, 0.0),
                                                            axis=0, keepdims=True)
                kn_ref[a * tk:(a + 1) * tk, :] = normed_keys(a)
                cb_ref[a] = key_decay(a)
                vta_ref[a], vtb_ref[a] = values_t(a)

        def logits(item, _):
            a, d = item
            cols = slice(a * tk, (a + 1) * tk)
            kn = kn_ref[d * tk:(d + 1) * tk, :]
            zs = []
            for sub, qt_ref in enumerate(qt_refs):
                decay = cb_ref[d, :, sub * LANES:(sub + 1) * LANES]
                z = (_dot(kn, qt_ref[:, cols]) - jnp.concatenate([decay] * (tk // LANES), axis=1)
                     + ct_ref[sub:sub + 1, cols])
                zs.append(jnp.where(causal, z, NEG_BIG) if d == a else z)
            return zs

        def probs(item, zs):
            return [jnp.exp2(z).astype(BF16) for z in zs]

        def values(item, ps):
            a, d = item
            cols = slice(a * tk, (a + 1) * tk)
            for acc_ref, vt_ref, p in zip(acc_refs, (vta_ref, vtb_ref), ps):
                if d == a:
                    acc_ref[:, cols] = _dot(vt_ref[d], p)
                else:
                    acc_ref[:, cols] += _dot(vt_ref[d], p)
            if d == 0:
                finalize(a)

        items = [(a, d) for a in range(nblk) for d in _descending(a + 1)]
        _pipeline(items, (prepare, logits, probs, values))

    def running_max_sweep():
        m = [[None] * nblk for _ in range(2)]

        def queries(d, _):
            prepare_queries(d)

        def logits(d, _):
            kn = normed_keys(d)
            cb = key_decay(d)
            zs = []
            for sub, qt_ref in enumerate(qt_refs):
                decay = cb[:, sub * LANES:(sub + 1) * LANES]
                z = _dot(kn, qt_ref[:, d * tk:]) - jnp.concatenate([decay] * ((s - d * tk) // LANES), axis=1)
                tri = jnp.where(causal, z[:, :tk], NEG_BIG)
                zs.append(tri if d == nblk - 1 else jnp.concatenate([tri, z[:, tk:]], axis=1))
            return zs

        def probs(d, zs):
            return _softmax_probs(d, zs, m, tk)

        def values(d, ps):
            _accumulate(ps, acc_refs, values_t(d), tk)

        _pipeline(_descending(nblk), (queries, logits, probs, values))
        for a in range(nblk):
            finalize(a)

    is_bounded = _qk_logit_bound(gq_ref, gk_ref) <= SAFE_LOGIT_BITS
    pl.when(is_bounded)(bounded_sweep)
    pl.when(jnp.logical_not(is_bounded))(running_max_sweep)


def _block_diag_ones():
    blk = lax.broadcasted_iota(jnp.int32, (LANES, LANES), 0) // HEAD_DIM
    return (blk == lax.broadcasted_iota(jnp.int32, (LANES, LANES), 1) // HEAD_DIM).astype(BF16)


def _fox_call(proj, c, gq, gk, n_pairs, q_col, k_col, v_col):
    b, s, _ = proj.shape
    tk = _key_tile(s)
    tok = lambda col: pl.BlockSpec((1, s, LANES), lambda i, p: (i, 0, col + p))
    const = lambda shape: pl.BlockSpec(shape, lambda i, p: (0,) * len(shape))
    rows_aug = HEAD_DIM + ONES_ROWS
    return pl.pallas_call(
        _fox_kernel, grid=(b, n_pairs),
        in_specs=[tok(q_col), tok(k_col), tok(v_col), pl.BlockSpec((1, s, LANES), lambda i, p: (i, 0, 0)),
                  const((1, LANES)), const((1, LANES)), const((LANES, LANES))],
        out_specs=tok(0),
        out_shape=jax.ShapeDtypeStruct((b, s, n_pairs * LANES), BF16),
        scratch_shapes=[pltpu.VMEM((LANES, s), BF16), pltpu.VMEM((LANES, s), BF16),
                        pltpu.VMEM((rows_aug, s), F32), pltpu.VMEM((rows_aug, s), F32), pltpu.VMEM((8, s), F32),
                        pltpu.VMEM((s, LANES), BF16),
                        pltpu.VMEM((s // tk, rows_aug, tk), BF16), pltpu.VMEM((s // tk, rows_aug, tk), BF16),
                        pltpu.VMEM((s // tk, tk, 2 * LANES), F32)],
        compiler_params=_params(2), name="fox_attn")(proj, proj, proj, c, gq, gk, _block_diag_ones())


def _diff_kernel(q_ref, k_ref, v_ref, bias_ref, gq_ref, gk_ref, bd_ref, lam_ref, sg_ref, o_ref,
                 q1t_ref, q2t_ref, acc1_ref, acc2_ref, kn_ref, vt_ref, *, lam_init):
    s = q_ref.shape[1]
    tk = bias_ref.shape[2]
    nblk = s // tk
    dv = v_ref.shape[2]
    causal = lax.broadcasted_iota(jnp.int32, (tk, tk), 0) <= lax.broadcasted_iota(jnp.int32, (tk, tk), 1)
    lam = (jnp.exp(jnp.sum(lam_ref[0:1, :] * lam_ref[1:2, :], axis=-1, keepdims=True))
           - jnp.exp(jnp.sum(lam_ref[2:3, :] * lam_ref[3:4, :], axis=-1, keepdims=True)) + lam_init)
    out_gain = sg_ref[...] * (1.0 - lam_init)

    def prepare_queries(a):
        qn = _group_rms(q_ref[0, a * tk:(a + 1) * tk, :].astype(F32), bd_ref[...], gq_ref[...])
        _store_split_qt(qn * (HEAD_DIM ** -0.5 * LOG2E), q1t_ref, q2t_ref, a * tk)

    def normed_keys(d):
        return _group_rms(k_ref[0, d * tk:(d + 1) * tk, :].astype(F32), bd_ref[...], gk_ref[...]).astype(BF16)

    def values_t(d):
        return _with_ones(v_ref[0, d * tk:(d + 1) * tk, :].astype(F32).T)

    def biased(z, a, d):
        if d == a:
            return jnp.where(causal, z + bias_ref[0, 0], NEG_BIG)
        return z + bias_ref[0, 1] if d == a - 1 else z

    def finalize(a):
        for c in range(a * tk // LANES, (a + 1) * tk // LANES):
            cols = slice(c * LANES, (c + 1) * LANES)
            o1 = acc1_ref[0:dv, cols] / acc1_ref[dv:dv + 1, cols]
            o2 = acc2_ref[0:dv, cols] / acc2_ref[dv:dv + 1, cols]
            o = o1 - lam * o2
            o = o * lax.rsqrt(jnp.mean(o * o, axis=0, keepdims=True) + RMS_EPS)
            o_ref[0, cols, :] = (o.T * out_gain).astype(o_ref.dtype)

    def bounded_sweep():
        def prepare(item, _):
            a, d = item
            if d == a:
                prepare_queries(a)
                kn_ref[a * tk:(a + 1) * tk, :] = normed_keys(a)
                vt_ref[a] = values_t(a)

        def logits(item, _):
            a, d = item
            kn = kn_ref[d * tk:(d + 1) * tk, :]
            return [biased(_dot(kn, qt_ref[:, a * tk:(a + 1) * tk]), a, d) for qt_ref in (q1t_ref, q2t_ref)]

        def probs(item, zs):
            return [jnp.exp2(z).astype(BF16) for z in zs]

        def values(item, ps):
            a, d = item
            cols = slice(a * tk, (a + 1) * tk)
            for acc_ref, p in zip((acc1_ref, acc2_ref), ps):
                if d == a:
                    acc_ref[:, cols] = _dot(vt_ref[d], p)
                else:
                    acc_ref[:, cols] += _dot(vt_ref[d], p)
            if d == 0:
                finalize(a)

        items = [(a, d) for a in range(nblk) for d in _descending(a + 1)]
        _pipeline(items, (prepare, logits, probs, values))

    def running_max_sweep():
        m = [[None] * nblk for _ in range(2)]

        def queries(d, _):
            prepare_queries(d)

        def logits(d, _):
            kn = normed_keys(d)
            zs = []
            for qt_ref in (q1t_ref, q2t_ref):
                z = _dot(kn, qt_ref[:, d * tk:])
                zs.append(jnp.concatenate([biased(z[:, (a - d) * tk:(a - d + 1) * tk], a, d) for a in range(d, min(d + 2, nblk))]
                                          + ([z[:, 2 * tk:]] if d < nblk - 2 else []), axis=1))
            return zs

        def probs(d, zs):
            return _softmax_probs(d, zs, m, tk)

        def values(d, ps):
            vt = values_t(d)
            _accumulate(ps, (acc1_ref, acc2_ref), (vt, vt), tk)

        _pipeline(_descending(nblk), (queries, logits, probs, values))
        for a in range(nblk):
            finalize(a)

    is_bounded = _qk_logit_bound(gq_ref, gk_ref) + jnp.max(jnp.abs(bias_ref[...])) <= SAFE_LOGIT_BITS
    pl.when(is_bounded)(bounded_sweep)
    pl.when(jnp.logical_not(is_bounded))(running_max_sweep)


def _diff_call(proj, bias, gq, gk, lam_rows, subln_g, n_heads, lam_init):
    b, s, _ = proj.shape
    tk = _key_tile(s)
    tok = lambda col: pl.BlockSpec((1, s, LANES), lambda i, p: (i, 0, col + p))
    const = lambda shape: pl.BlockSpec(shape, lambda i, p: (0,) * len(shape))
    dv = 2 * HEAD_DIM
    assert dv == LANES
    rows_aug = dv + ONES_ROWS
    return pl.pallas_call(
        functools.partial(_diff_kernel, lam_init=lam_init), grid=(b, n_heads),
        in_specs=[tok(0), tok(n_heads), tok(2 * n_heads),
                  pl.BlockSpec((1, 2, tk, tk), lambda i, p: (p, 0, 0, 0)),
                  const((1, LANES)), const((1, LANES)), const((LANES, LANES)), const((8, LANES)), const((1, dv))],
        out_specs=tok(0),
        out_shape=jax.ShapeDtypeStruct((b, s, n_heads * dv), BF16),
        scratch_shapes=[pltpu.VMEM((LANES, s), BF16), pltpu.VMEM((LANES, s), BF16),
                        pltpu.VMEM((rows_aug, s), F32), pltpu.VMEM((rows_aug, s), F32),
                        pltpu.VMEM((s, LANES), BF16), pltpu.VMEM((s // tk, rows_aug, tk), BF16)],
        compiler_params=_params(2), name="diff_attn")(proj, proj, proj, bias, gq, gk, _block_diag_ones(),
                                                      lam_rows, subln_g)


def _t5_bucket(dist):
    max_exact = N_BUCKETS // 2
    nf = jnp.maximum(dist, 1).astype(F32)
    large = max_exact + (jnp.log(nf / max_exact) / math.log(MAX_DISTANCE / max_exact)
                         * (N_BUCKETS - max_exact)).astype(jnp.int32)
    large = jnp.minimum(large, N_BUCKETS - 1)
    return jnp.where(dist < max_exact, dist, large)


def _bias_tiles(rel_bias, t):
    assert t >= MAX_DISTANCE
    table = rel_bias.astype(F32)
    krow = lax.broadcasted_iota(jnp.int32, (2, t, t), 1)
    qcol = lax.broadcasted_iota(jnp.int32, (2, t, t), 2)
    dist = qcol - krow + t * lax.broadcasted_iota(jnp.int32, (2, t, t), 0)
    onehot = (_t5_bucket(jnp.maximum(dist, 0))[..., None] == jnp.arange(N_BUCKETS)).astype(F32)
    return jnp.einsum('otkb,bh->hotk', onehot, (table - table[N_BUCKETS - 1]) * LOG2E,
                      precision=lax.Precision.HIGHEST)


def _post_kernel(*refs, n_mix):
    x_ref = refs[0]
    o_refs = refs[1:1 + n_mix]
    wo_ref, g_ref, wg_ref, wu_ref, wd_ref, out_ref, x1_ref, act_ref = refs[1 + n_mix:]
    o = jnp.concatenate([o_ref[0] for o_ref in o_refs], axis=1) if n_mix > 1 else o_refs[0][0]
    d = x_ref.shape[2]
    for lo in range(0, d, PROJ_COL_TILE):
        cols = slice(lo, lo + PROJ_COL_TILE)
        x1_ref[:, cols] = x_ref[0, :, cols] + _dot(o, wo_ref[:, cols])
    h = _rms_rows(x1_ref[...], g_ref[...]).astype(BF16)
    d_ff = wg_ref.shape[1]
    for lo in range(0, d_ff, FFN_COL_TILE):
        cols = slice(lo, min(lo + FFN_COL_TILE, d_ff))
        a = _dot(h, wg_ref[:, cols])
        u = _dot(h, wu_ref[:, cols])
        act_ref[:, cols] = (a * (1.0 / (1.0 + jnp.exp(-a))) * u).astype(BF16)
    for lo in range(0, d, PROJ_COL_TILE):
        cols = slice(lo, lo + PROJ_COL_TILE)
        out_ref[0, :, cols] = x1_ref[:, cols] + _dot(act_ref[...], wd_ref[:, cols])


def _post_call(x, mixes, w_out, g, wg, wu, wd):
    b, s, d = x.shape
    tm = min(TOKEN_TILE, s)
    n_mix = len(mixes)
    tok = lambda width: pl.BlockSpec((1, tm, width), lambda i, j: (i, j, 0))
    res = lambda shape: pl.BlockSpec(shape, lambda i, j: (0, 0), pipeline_mode=pl.Buffered(1))
    return pl.pallas_call(
        functools.partial(_post_kernel, n_mix=n_mix), grid=(b, s // tm),
        in_specs=([tok(d)] + [tok(m.shape[2]) for m in mixes]
                  + [res(w_out.shape), res((1, d)), res(wg.shape), res(wu.shape), res(wd.shape)]),
        out_specs=tok(d),
        out_shape=jax.ShapeDtypeStruct((b, s, d), x.dtype),
        scratch_shapes=[pltpu.VMEM((tm, d), F32), pltpu.VMEM((tm, wg.shape[1]), BF16)],
        compiler_params=_params(2), name="post")(x, *mixes, w_out, g, wg, wu, wd)


def _pair_gain(g):
    return jnp.concatenate([g, g]).astype(F32)[None, :]


def kernel(x, attn_norm_g, ffn_norm_g, even_w_in, fox_forget_b, fox_q_norm_g, fox_k_norm_g, even_w_out, diff_w_in, diff_q_norm_g, diff_k_norm_g, diff_lambda_q1, diff_lambda_k1, diff_lambda_q2, diff_lambda_k2, diff_subln_g, diff_w_out, rel_bias, ffn_w_gate, ffn_w_up, ffn_w_down):
    b, s, d = x.shape
    depth = attn_norm_g.shape[0]
    n_sb = d // (2 * HEAD_DIM)
    n_fox = d // (2 * HEAD_DIM)
    n_diff = d // (2 * HEAD_DIM)
    sb_w = n_sb * HEAD_DIM
    fox_w = n_fox * HEAD_DIM
    main_w = 3 * sb_w + 3 * fox_w
    assert sb_w % LANES == 0 and fox_w % LANES == 0 and n_fox <= LANES
    bias = _bias_tiles(rel_bias, _key_tile(s))

    for layer in range(depth):
        g_attn = attn_norm_g[layer][None, :]
        if layer % 2 == 0:
            e = layer // 2
            w_in = even_w_in[e]
            wfg = jnp.pad(w_in[:, main_w:], ((0, 0), (0, LANES - n_fox))).astype(BF16)
            fb = jnp.pad(fox_forget_b[e].astype(F32), (0, LANES - n_fox))[None, :]
            proj, c = _pre_call(x, g_attn, w_in[:, :main_w].astype(BF16), (wfg, fb))
            sbc, fxc = sb_w // LANES, fox_w // LANES
            o_a = _sb_call(proj, sbc, 0, sbc, 2 * sbc)
            o_b = _fox_call(proj, c, _pair_gain(fox_q_norm_g[e]), _pair_gain(fox_k_norm_g[e]),
                            fxc, 3 * sbc, 3 * sbc + fxc, 3 * sbc + 2 * fxc)
            mixes, w_out = [o_a, o_b], even_w_out[e].astype(BF16)
        else:
            o = layer // 2
            proj = _pre_call(x, g_attn, diff_w_in[o].astype(BF16))
            lam_rows = jnp.stack([diff_lambda_q1[o], diff_lambda_k1[o], diff_lambda_q2[o], diff_lambda_k2[o]])
            lam_rows = jnp.pad(lam_rows.astype(F32), ((0, 4), (0, LANES - HEAD_DIM)))
            lam_init = 0.8 - 0.6 * math.exp(-0.3 * layer)
            mix = _diff_call(proj, bias, _pair_gain(diff_q_norm_g[o]), _pair_gain(diff_k_norm_g[o]),
                             lam_rows, diff_subln_g[o].astype(F32)[None, :], n_diff, lam_init)
            mixes, w_out = [mix], diff_w_out[o].astype(BF16)
        x = _post_call(x, mixes, w_out, ffn_norm_g[layer][None, :], ffn_w_gate[layer].astype(BF16),
                       ffn_w_up[layer].astype(BF16), ffn_w_down[layer].astype(BF16))
    return x
```

```python
import functools
import math

import jax
import jax.numpy as jnp
import numpy as np
from jax import lax
from jax.experimental import pallas as pl
from jax.experimental.pallas import tpu as pltpu

F32 = jnp.float32
BF16 = jnp.bfloat16

HEAD_DIM = 64
N_BUCKETS = 32
MAX_DISTANCE = 128
RMS_EPS = 1e-6
LOG2E = 1.4426950408889634
NEG_BIG = -1e30
SIGN_BIT = np.uint32(0x80000000)

LANES = 128
K_TILE = 256
TOKEN_TILE = 512
PROJ_COL_TILE = 512
FFN_COL_TILE = 512
ROW_SPLIT = 2
ONES_ROWS = 16
SB_NEAR_TILES = 2
SB_ZERO_BITS = 152.0
SAFE_LOGIT_BITS = 30.0
QK_BOUND_SLACK = 1.01
VMEM_LIMIT = 56 * 1024 * 1024


def _dot(a, b):
    return jnp.dot(a, b, preferred_element_type=F32)


def _params(n_axes, vmem=VMEM_LIMIT):
    return pltpu.CompilerParams(dimension_semantics=("arbitrary",) * n_axes, vmem_limit_bytes=vmem)


def _rms_rows(x, g):
    ms = jnp.mean(x * x, axis=-1, keepdims=True)
    return x * lax.rsqrt(ms + RMS_EPS) * g


def _split3(x):
    hi = x.astype(BF16)
    r = x - hi.astype(F32)
    mid = r.astype(BF16)
    lo = (r - mid.astype(F32)).astype(BF16)
    return hi, mid, lo


def _row_groups(tm):
    step = tm // ROW_SPLIT
    return [slice(r, r + step) for r in range(0, tm, step)]


def _project(h, rows, w_ref, o_ref, col_tiles):
    for j in col_tiles:
        cols = slice(j * PROJ_COL_TILE, (j + 1) * PROJ_COL_TILE)
        o_ref[0, rows, cols] = _dot(h, w_ref[:, cols]).astype(o_ref.dtype)


def _pre_odd_kernel(x_ref, g_ref, w_ref, o_ref):
    groups = _row_groups(x_ref.shape[1])
    hs = [_rms_rows(x_ref[0, rows, :], g_ref[...]).astype(BF16) for rows in groups]
    for rows, h in zip(groups, hs):
        _project(h, rows, w_ref, o_ref, range(w_ref.shape[1] // PROJ_COL_TILE))


def _pre_even_kernel(x_ref, g_ref, w_ref, wfg_ref, fb_ref, ltri_ref, o_ref, c_ref, carry_ref):
    @pl.when(pl.program_id(1) == 0)
    def _():
        carry_ref[...] = jnp.zeros_like(carry_ref)

    groups = _row_groups(x_ref.shape[1])
    n_tiles = w_ref.shape[1] // PROJ_COL_TILE
    tc = ltri_ref.shape[0]
    assert groups[0].stop % tc == 0
    hs = [_rms_rows(x_ref[0, rows, :], g_ref[...]).astype(BF16) for rows in groups]
    pieces = []
    for rows, h in zip(groups, hs):
        fg = _dot(h, wfg_ref[...]) + fb_ref[...]
        logf = jnp.minimum(fg, 0.0) - jnp.log1p(jnp.exp(-jnp.abs(fg)))
        pieces.append([_split3(logf[r:r + tc]) for r in range(0, logf.shape[0], tc)])
    for rows, h in zip(groups, hs):
        _project(h, rows, w_ref, o_ref, range(n_tiles // 2))
    carry = carry_ref[...]
    for rows, chunks in zip(groups, pieces):
        for ci, (hi, mid, lo) in enumerate(chunks):
            cs = _dot(ltri_ref[...], hi) + _dot(ltri_ref[...], mid) + _dot(ltri_ref[...], lo) + carry
            c_ref[0, rows.start + ci * tc:rows.start + (ci + 1) * tc, :] = cs
            carry = cs[tc - 1:tc, :]
    carry_ref[...] = carry
    for rows, h in zip(groups, hs):
        _project(h, rows, w_ref, o_ref, range(n_tiles // 2, n_tiles))


def _pre_call(x, g, w, extra=None):
    b, s, d = x.shape
    n = w.shape[1]
    tm = min(TOKEN_TILE, s)
    grid = (b, s // tm)
    x_spec = pl.BlockSpec((1, tm, d), lambda i, j: (i, j, 0))
    g_spec = pl.BlockSpec((1, d), lambda i, j: (0, 0))
    w_spec = pl.BlockSpec((d, n), lambda i, j: (0, 0))
    o_spec = pl.BlockSpec((1, tm, n), lambda i, j: (i, j, 0))
    o_shape = jax.ShapeDtypeStruct((b, s, n), BF16)
    if extra is None:
        return pl.pallas_call(
            _pre_odd_kernel, grid=grid, in_specs=[x_spec, g_spec, w_spec], out_specs=o_spec,
            out_shape=o_shape, compiler_params=_params(2), name="pre_odd")(x, g, w)
    wfg, fb = extra
    tc = min(K_TILE, tm)
    ltri = (lax.broadcasted_iota(jnp.int32, (tc, tc), 0) >= lax.broadcasted_iota(jnp.int32, (tc, tc), 1)).astype(BF16)
    return pl.pallas_call(
        _pre_even_kernel, grid=grid,
        in_specs=[x_spec, g_spec, w_spec,
                  pl.BlockSpec((d, LANES), lambda i, j: (0, 0)),
                  pl.BlockSpec((1, LANES), lambda i, j: (0, 0)),
                  pl.BlockSpec((tc, tc), lambda i, j: (0, 0))],
        out_specs=[o_spec, pl.BlockSpec((1, tm, LANES), lambda i, j: (i, j, 0))],
        out_shape=[o_shape, jax.ShapeDtypeStruct((b, s, LANES), F32)],
        scratch_shapes=[pltpu.VMEM((1, LANES), F32)],
        compiler_params=_params(2), name="pre_even")(x, g, w, wfg, fb, ltri)


def _key_tile(s):
    tk = min(K_TILE, s)
    assert s % tk == 0 and tk % LANES == 0
    return tk


def _pipeline(items, stages, lag=1):
    carried = [{} for _ in stages]
    for step in range(len(items) + lag * (len(stages) - 1)):
        for i, stage in enumerate(stages):
            idx = step - i * lag
            if 0 <= idx < len(items):
                arg = carried[i - 1].pop(idx) if i else None
                carried[i][idx] = stage(items[idx], arg)


def _descending(n):
    return list(reversed(range(n)))


def _store_split_qt(qn, qat_ref, qbt_ref, col0):
    row = lax.broadcasted_iota(jnp.int32, (LANES, LANES), 0)
    for c in range(qn.shape[0] // LANES):
        cols = slice(col0 + c * LANES, col0 + (c + 1) * LANES)
        qt = qn[c * LANES:(c + 1) * LANES].T
        qat_ref[:, cols] = jnp.where(row < HEAD_DIM, qt, 0.0).astype(BF16)
        qbt_ref[:, cols] = jnp.where(row >= HEAD_DIM, qt, 0.0).astype(BF16)


def _group_rms(x, bd, g):
    ss = _dot((x * x).astype(BF16), bd)
    return x * lax.rsqrt(ss * (1.0 / HEAD_DIM) + RMS_EPS) * g


def _softmax_probs(d, zs, m, tk):
    out = []
    for cb in range(zs[0].shape[1] // tk):
        a = d + cb
        for i, z in enumerate(zs):
            zb = z[:, cb * tk:(cb + 1) * tk]
            m_cur = jnp.max(zb, axis=0, keepdims=True)
            if a == d:
                out.append((a, i, jnp.exp2(zb - m_cur).astype(BF16), None))
                m[i][a] = m_cur
            else:
                m_new = jnp.maximum(m[i][a], m_cur)
                out.append((a, i, jnp.exp2(zb - m_new).astype(BF16), jnp.exp2(m[i][a] - m_new)))
                m[i][a] = m_new
    return out


def _accumulate(probs, acc_refs, vts, tk):
    for a, i, p, alpha in probs:
        cols = slice(a * tk, (a + 1) * tk)
        if alpha is None:
            acc_refs[i][:, cols] = _dot(vts[i], p)
        else:
            acc_refs[i][:, cols] = acc_refs[i][:, cols] * alpha + _dot(vts[i], p)


def _with_ones(vt):
    return jnp.concatenate([vt.astype(BF16), jnp.ones((ONES_ROWS, vt.shape[1]), BF16)], axis=0)


def _sb_kernel(q_ref, k_ref, v_ref, u_ref, o_ref, qat_ref, qbt_ref, acc_ref, r_ref):
    s = q_ref.shape[1]
    tk = u_ref.shape[0]
    nblk = s // tk
    strict = lax.broadcasted_iota(jnp.int32, (tk, tk), 0) < lax.broadcasted_iota(jnp.int32, (tk, tk), 1)
    r = [[None] * nblk for _ in range(2)]

    def suffix_sum(z, tri):
        neg_abs = lax.bitcast_convert_type(lax.bitcast_convert_type(z, jnp.uint32) | SIGN_BIT, F32)
        sp = jnp.maximum(z, 0.0) + LOG2E * jnp.log(1.0 + jnp.exp2(neg_abs))
        if tri:
            sp = jnp.where(strict, sp, 0.0)
        cum = _dot(u_ref[...], sp.astype(BF16))
        return z - sp, cum, cum[0:1, :] + sp[0:1, :]

    def queries(d, _):
        qs = q_ref[0, d * tk:(d + 1) * tk, :].astype(F32) * (HEAD_DIM ** -0.5 * LOG2E)
        _store_split_qt(qs, qat_ref, qbt_ref, d * tk)

    def logits(d, _):
        kblk = k_ref[0, d * tk:(d + 1) * tk, :]
        cols = slice(d * tk, min(d + SB_NEAR_TILES, nblk) * tk)
        return [_dot(kblk, qt_ref[:, cols]) for qt_ref in (qat_ref, qbt_ref)]

    def suffix_sums(d, zs):
        chains = []
        for cb in range(zs[0].shape[1] // tk):
            for sub in range(2):
                chains.append((cb, sub) + suffix_sum(zs[sub][:, cb * tk:(cb + 1) * tk], cb == 0))
        return chains

    def values(d, chains):
        vt = v_ref[0, d * tk:(d + 1) * tk, :].astype(F32).T.astype(BF16)
        for cb, sub, log_beta, cum, total in chains:
            a = d + cb
            rows = slice(sub * HEAD_DIM, (sub + 1) * HEAD_DIM)
            cols = slice(a * tk, (a + 1) * tk)
            if cb == 0:
                w = jnp.where(strict, jnp.exp2(log_beta - cum), 0.0)
                acc_ref[rows, cols] = _dot(vt[rows], w.astype(BF16))
                r[sub][a] = total
            else:
                w = jnp.exp2(log_beta - cum - r[sub][a])
                acc_ref[rows, cols] += _dot(vt[rows], w.astype(BF16))
                r[sub][a] = r[sub][a] + total

    _pipeline(_descending(nblk), (queries, logits, suffix_sums, values))

    def far_tile(d):
        kblk = k_ref[0, d * tk:(d + 1) * tk, :]
        vt = v_ref[0, d * tk:(d + 1) * tk, :].astype(F32).T.astype(BF16)
        for a in range(d + SB_NEAR_TILES, nblk):
            cols = slice(a * tk, (a + 1) * tk)
            for sub, qt_ref in enumerate((qat_ref, qbt_ref)):
                rows = slice(sub * HEAD_DIM, (sub + 1) * HEAD_DIM)
                log_beta, cum, total = suffix_sum(_dot(kblk, qt_ref[:, cols]), False)
                r_old = r_ref[sub:sub + 1, cols]
                acc_ref[rows, cols] += _dot(vt[rows], jnp.exp2(log_beta - cum - r_old).astype(BF16))
                r_ref[sub:sub + 1, cols] = r_old + total

    if nblk > SB_NEAR_TILES:
        for sub in range(2):
            r_ref[sub:sub + 1, :] = jnp.concatenate(r[sub], axis=1)
        for d in reversed(range(nblk - SB_NEAR_TILES)):
            @pl.when(jnp.min(r_ref[0:2, (d + SB_NEAR_TILES) * tk:]) < SB_ZERO_BITS)
            def _(d=d):
                far_tile(d)

    for c in range(s // LANES):
        cols = slice(c * LANES, (c + 1) * LANES)
        o_ref[0, cols, :] = acc_ref[:, cols].T.astype(o_ref.dtype)


def _sb_call(proj, n_pairs, q_col, k_col, v_col):
    b, s, _ = proj.shape
    tk = _key_tile(s)
    u = (lax.broadcasted_iota(jnp.int32, (tk, tk), 1) > lax.broadcasted_iota(jnp.int32, (tk, tk), 0)).astype(BF16)
    tok = lambda col: pl.BlockSpec((1, s, LANES), lambda i, p: (i, 0, col + p))
    return pl.pallas_call(
        _sb_kernel, grid=(b, n_pairs),
        in_specs=[tok(q_col), tok(k_col), tok(v_col), pl.BlockSpec((tk, tk), lambda i, p: (0, 0))],
        out_specs=tok(0),
        out_shape=jax.ShapeDtypeStruct((b, s, n_pairs * LANES), BF16),
        scratch_shapes=[pltpu.VMEM((LANES, s), BF16), pltpu.VMEM((LANES, s), BF16), pltpu.VMEM((LANES, s), F32),
                        pltpu.VMEM((8, s), F32)],
        compiler_params=_params(2), name="sb_attn")(proj, proj, proj, u)


def _qk_logit_bound(gq_ref, gk_ref):
    gains = jnp.max(jnp.abs(gq_ref[...])) * jnp.max(jnp.abs(gk_ref[...]))
    return gains * (HEAD_DIM ** 0.5 * LOG2E * QK_BOUND_SLACK)


def _fox_kernel(q_ref, k_ref, v_ref, c_ref, gq_ref, gk_ref, bd_ref, o_ref,
                qat_ref, qbt_ref, acca_ref, accb_ref, ct_ref, kn_ref, vta_ref, vtb_ref, cb_ref):
    s = q_ref.shape[1]
    tk = _key_tile(s)
    nblk = s // tk
    p_idx = pl.program_id(1)
    sel_r = lax.broadcasted_iota(jnp.int32, (LANES, 2 * LANES), 0)
    sel_l = lax.broadcasted_iota(jnp.int32, (LANES, 2 * LANES), 1)
    sel = (sel_r == 2 * p_idx + (sel_l >= LANES).astype(jnp.int32)).astype(BF16)
    causal = lax.broadcasted_iota(jnp.int32, (tk, tk), 0) <= lax.broadcasted_iota(jnp.int32, (tk, tk), 1)
    head_row = lax.broadcasted_iota(jnp.int32, (LANES, LANES), 0)
    qt_refs, acc_refs = (qat_ref, qbt_ref), (acca_ref, accb_ref)

    def prepare_queries(a):
        qn = _group_rms(q_ref[0, a * tk:(a + 1) * tk, :].astype(F32), bd_ref[...], gq_ref[...])
        _store_split_qt(qn * (HEAD_DIM ** -0.5 * LOG2E), qat_ref, qbt_ref, a * tk)

    def normed_keys(d):
        return _group_rms(k_ref[0, d * tk:(d + 1) * tk, :].astype(F32), bd_ref[...], gk_ref[...]).astype(BF16)

    def key_decay(d):
        hi, mid, lo = _split3(c_ref[0, d * tk:(d + 1) * tk, :] * LOG2E)
        return _dot(hi, sel) + _dot(mid, sel) + _dot(lo, sel)

    def values_t(d):
        vt = v_ref[0, d * tk:(d + 1) * tk, :].astype(F32).T
        return _with_ones(vt[:HEAD_DIM]), _with_ones(vt[HEAD_DIM:])

    def finalize(a):
        for c in range(a * tk // LANES, (a + 1) * tk // LANES):
            cols = slice(c * LANES, (c + 1) * LANES)
            oa = acca_ref[0:HEAD_DIM, cols] / acca_ref[HEAD_DIM:HEAD_DIM + 1, cols]
            ob = accb_ref[0:HEAD_DIM, cols] / accb_ref[HEAD_DIM:HEAD_DIM + 1, cols]
            o_ref[0, cols, :] = jnp.concatenate([oa, ob], axis=0).T.astype(o_ref.dtype)

    def bounded_sweep():
        def prepare(item, _):
            a, d = item
            if d == a:
                prepare_queries(a)
                for c in range(a * tk // LANES, (a + 1) * tk // LANES):
                    cols = slice(c * LANES, (c + 1) * LANES)
                    ct = (c_ref[0, cols, :] * LOG2E).T
                    for sub in range(2):
                        ct_ref[sub:sub + 1, cols] = jnp.sum(jnp.where(head_row == 2 * p_idx + sub, ct, 0.0),
                                                            axis=0, keepdims=True)
                kn_ref[a * tk:(a + 1) * tk, :] = normed_keys(a)
                cb_ref[a] = key_decay(a)
                vta_ref[a], vtb_ref[a] = values_t(a)

        def logits(item, _):
            a, d = item
            cols = slice(a * tk, (a + 1) * tk)
            kn = kn_ref[d * tk:(d + 1) * tk, :]
            zs = []
            for sub, qt_ref in enumerate(qt_refs):
                decay = cb_ref[d, :, sub * LANES:(sub + 1) * LANES]
                z = (_dot(kn, qt_ref[:, cols]) - jnp.concatenate([decay] * (tk // LANES), axis=1)
                     + ct_ref[sub:sub + 1, cols])
                zs.append(jnp.where(causal, z, NEG_BIG) if d == a else z)
            return zs

        def probs(item, zs):
            return [jnp.exp2(z).astype(BF16) for z in zs]

        def values(item, ps):
            a, d = item
            cols = slice(a * tk, (a + 1) * tk)
            for acc_ref, vt_ref, p in zip(acc_refs, (vta_ref, vtb_ref), ps):
                if d == a:
                    acc_ref[:, cols] = _dot(vt_ref[d], p)
                else:
                    acc_ref[:, cols] += _dot(vt_ref[d], p)
            if d == 0:
                finalize(a)

        items = [(a, d) for a in range(nblk) for d in _descending(a + 1)]
        _pipeline(items, (prepare, logits, probs, values))

    def running_max_sweep():
        m = [[None] * nblk for _ in range(2)]

        def queries(d, _):
            prepare_queries(d)

        def logits(d, _):
            kn = normed_keys(d)
            cb = key_decay(d)
            zs = []
            for sub, qt_ref in enumerate(qt_refs):
                decay = cb[:, sub * LANES:(sub + 1) * LANES]
                z = _dot(kn, qt_ref[:, d * tk:]) - jnp.concatenate([decay] * ((s - d * tk) // LANES), axis=1)
                tri = jnp.where(causal, z[:, :tk], NEG_BIG)
                zs.append(tri if d == nblk - 1 else jnp.concatenate([tri, z[:, tk:]], axis=1))
            return zs

        def probs(d, zs):
            return _softmax_probs(d, zs, m, tk)

        def values(d, ps):
            _accumulate(ps, acc_refs, values_t(d), tk)

        _pipeline(_descending(nblk), (queries, logits, probs, values))
        for a in range(nblk):
            finalize(a)

    is_bounded = _qk_logit_bound(gq_ref, gk_ref) <= SAFE_LOGIT_BITS
    pl.when(is_bounded)(bounded_sweep)
    pl.when(jnp.logical_not(is_bounded))(running_max_sweep)


def _block_diag_ones():
    blk = lax.broadcasted_iota(jnp.int32, (LANES, LANES), 0) // HEAD_DIM
    return (blk == lax.broadcasted_iota(jnp.int32, (LANES, LANES), 1) // HEAD_DIM).astype(BF16)


def _fox_call(proj, c, gq, gk, n_pairs, q_col, k_col, v_col):
    b, s, _ = proj.shape
    tk = _key_tile(s)
    tok = lambda col: pl.BlockSpec((1, s, LANES), lambda i, p: (i, 0, col + p))
    const = lambda shape: pl.BlockSpec(shape, lambda i, p: (0,) * len(shape))
    rows_aug = HEAD_DIM + ONES_ROWS
    return pl.pallas_call(
        _fox_kernel, grid=(b, n_pairs),
        in_specs=[tok(q_col), tok(k_col), tok(v_col), pl.BlockSpec((1, s, LANES), lambda i, p: (i, 0, 0)),
                  const((1, LANES)), const((1, LANES)), const((LANES, LANES))],
        out_specs=tok(0),
        out_shape=jax.ShapeDtypeStruct((b, s, n_pairs * LANES), BF16),
        scratch_shapes=[pltpu.VMEM((LANES, s), BF16), pltpu.VMEM((LANES, s), BF16),
                        pltpu.VMEM((rows_aug, s), F32), pltpu.VMEM((rows_aug, s), F32), pltpu.VMEM((8, s), F32),
                        pltpu.VMEM((s, LANES), BF16),
                        pltpu.VMEM((s // tk, rows_aug, tk), BF16), pltpu.VMEM((s // tk, rows_aug, tk), BF16),
                        pltpu.VMEM((s // tk, tk, 2 * LANES), F32)],
        compiler_params=_params(2), name="fox_attn")(proj, proj, proj, c, gq, gk, _block_diag_ones())


def _diff_kernel(q_ref, k_ref, v_ref, bias_ref, gq_ref, gk_ref, bd_ref, lam_ref, sg_ref, o_ref,
                 q1t_ref, q2t_ref, acc1_ref, acc2_ref, kn_ref, vt_ref, *, lam_init):
    s = q_ref.shape[1]
    tk = bias_ref.shape[2]
    nblk = s // tk
    dv = v_ref.shape[2]
    causal = lax.broadcasted_iota(jnp.int32, (tk, tk), 0) <= lax.broadcasted_iota(jnp.int32, (tk, tk), 1)
    lam = (jnp.exp(jnp.sum(lam_ref[0:1, :] * lam_ref[1:2, :], axis=-1, keepdims=True))
           - jnp.exp(jnp.sum(lam_ref[2:3, :] * lam_ref[3:4, :], axis=-1, keepdims=True)) + lam_init)
    out_gain = sg_ref[...] * (1.0 - lam_init)

    def prepare_queries(a):
        qn = _group_rms(q_ref[0, a * tk:(a + 1) * tk, :].astype(F32), bd_ref[...], gq_ref[...])
        _store_split_qt(qn * (HEAD_DIM ** -0.5 * LOG2E), q1t_ref, q2t_ref, a * tk)

    def normed_keys(d):
        return _group_rms(k_ref[0, d * tk:(d + 1) * tk, :].astype(F32), bd_ref[...], gk_ref[...]).astype(BF16)

    def values_t(d):
        return _with_ones(v_ref[0, d * tk:(d + 1) * tk, :].astype(F32).T)

    def biased(z, a, d):
        if d == a:
            return jnp.where(causal, z + bias_ref[0, 0], NEG_BIG)
        return z + bias_ref[0, 1] if d == a - 1 else z

    def finalize(a):
        for c in range(a * tk // LANES, (a + 1) * tk // LANES):
            cols = slice(c * LANES, (c + 1) * LANES)
            o1 = acc1_ref[0:dv, cols] / acc1_ref[dv:dv + 1, cols]
            o2 = acc2_ref[0:dv, cols] / acc2_ref[dv:dv + 1, cols]
            o = o1 - lam * o2
            o = o * lax.rsqrt(jnp.mean(o * o, axis=0, keepdims=True) + RMS_EPS)
            o_ref[0, cols, :] = (o.T * out_gain).astype(o_ref.dtype)

    def bounded_sweep():
        def prepare(item, _):
            a, d = item
            if d == a:
                prepare_queries(a)
                kn_ref[a * tk:(a + 1) * tk, :] = normed_keys(a)
                vt_ref[a] = values_t(a)

        def logits(item, _):
            a, d = item
            kn = kn_ref[d * tk:(d + 1) * tk, :]
            return [biased(_dot(kn, qt_ref[:, a * tk:(a + 1) * tk]), a, d) for qt_ref in (q1t_ref, q2t_ref)]

        def probs(item, zs):
            return [jnp.exp2(z).astype(BF16) for z in zs]

        def values(item, ps):
            a, d = item
            cols = slice(a * tk, (a + 1) * tk)
            for acc_ref, p in zip((acc1_ref, acc2_ref), ps):
                if d == a:
                    acc_ref[:, cols] = _dot(vt_ref[d], p)
                else:
                    acc_ref[:, cols] += _dot(vt_ref[d], p)
            if d == 0:
                finalize(a)

        items = [(a, d) for a in range(nblk) for d in _descending(a + 1)]
        _pipeline(items, (prepare, logits, probs, values), lag=3)

    def running_max_sweep():
        m = [[None] * nblk for _ in range(2)]

        def queries(d, _):
            prepare_queries(d)

        def logits(d, _):
            kn = normed_keys(d)
            zs = []
            for qt_ref in (q1t_ref, q2t_ref):
                z = _dot(kn, qt_ref[:, d * tk:])
                zs.append(jnp.concatenate([biased(z[:, (a - d) * tk:(a - d + 1) * tk], a, d) for a in range(d, min(d + 2, nblk))]
                                          + ([z[:, 2 * tk:]] if d < nblk - 2 else []), axis=1))
            return zs

        def probs(d, zs):
            return _softmax_probs(d, zs, m, tk)

        def values(d, ps):
            vt = values_t(d)
            _accumulate(ps, (acc1_ref, acc2_ref), (vt, vt), tk)

        _pipeline(_descending(nblk), (queries, logits, probs, values))
        for a in range(nblk):
            finalize(a)

    is_bounded = _qk_logit_bound(gq_ref, gk_ref) + jnp.max(jnp.abs(bias_ref[...])) <= SAFE_LOGIT_BITS
    pl.when(is_bounded)(bounded_sweep)
    pl.when(jnp.logical_not(is_bounded))(running_max_sweep)


def _diff_call(proj, bias, gq, gk, lam_rows, subln_g, n_heads, lam_init):
    b, s, _ = proj.shape
    tk = _key_tile(s)
    tok = lambda col: pl.BlockSpec((1, s, LANES), lambda i, p: (i, 0, col + p))
    const = lambda shape: pl.BlockSpec(shape, lambda i, p: (0,) * len(shape))
    dv = 2 * HEAD_DIM
    assert dv == LANES
    rows_aug = dv + ONES_ROWS
    return pl.pallas_call(
        functools.partial(_diff_kernel, lam_init=lam_init), grid=(b, n_heads),
        in_specs=[tok(0), tok(n_heads), tok(2 * n_heads),
                  pl.BlockSpec((1, 2, tk, tk), lambda i, p: (p, 0, 0, 0)),
                  const((1, LANES)), const((1, LANES)), const((LANES, LANES)), const((8, LANES)), const((1, dv))],
        out_specs=tok(0),
        out_shape=jax.ShapeDtypeStruct((b, s, n_heads * dv), BF16),
        scratch_shapes=[pltpu.VMEM((LANES, s), BF16), pltpu.VMEM((LANES, s), BF16),
                        pltpu.VMEM((rows_aug, s), F32), pltpu.VMEM((rows_aug, s), F32),
                        pltpu.VMEM((s, LANES), BF16), pltpu.VMEM((s // tk, rows_aug, tk), BF16)],
        compiler_params=_params(2), name="diff_attn")(proj, proj, proj, bias, gq, gk, _block_diag_ones(),
                                                      lam_rows, subln_g)


def _t5_bucket(dist):
    max_exact = N_BUCKETS // 2
    nf = jnp.maximum(dist, 1).astype(F32)
    large = max_exact + (jnp.log(nf / max_exact) / math.log(MAX_DISTANCE / max_exact)
                         * (N_BUCKETS - max_exact)).astype(jnp.int32)
    large = jnp.minimum(large, N_BUCKETS - 1)
    return jnp.where(dist < max_exact, dist, large)


def _bias_tiles(rel_bias, t):
    assert t >= MAX_DISTANCE
    table = rel_bias.astype(F32)
    krow = lax.broadcasted_iota(jnp.int32, (2, t, t), 1)
    qcol = lax.broadcasted_iota(jnp.int32, (2, t, t), 2)
    dist = qcol - krow + t * lax.broadcasted_iota(jnp.int32, (2, t, t), 0)
    onehot = (_t5_bucket(jnp.maximum(dist, 0))[..., None] == jnp.arange(N_BUCKETS)).astype(F32)
    return jnp.einsum('otkb,bh->hotk', onehot, (table - table[N_BUCKETS - 1]) * LOG2E,
                      precision=lax.Precision.HIGHEST)


def _post_kernel(*refs, n_mix):
    x_ref = refs[0]
    o_refs = refs[1:1 + n_mix]
    wo_ref, g_ref, wg_ref, wu_ref, wd_ref, out_ref, x1_ref, act_ref = refs[1 + n_mix:]
    tm, d = x_ref.shape[1:]
    d_ff = wg_ref.shape[1]
    halves = _row_groups(tm)
    for rows in halves:
        o = jnp.concatenate([o_ref[0, rows, :] for o_ref in o_refs], axis=1) if n_mix > 1 else o_refs[0][0, rows, :]
        for lo in range(0, d, PROJ_COL_TILE):
            cols = slice(lo, lo + PROJ_COL_TILE)
            x1_ref[rows, cols] = x_ref[0, rows, cols] + _dot(o, wo_ref[:, cols])
    for rows in halves:
        h = _rms_rows(x1_ref[rows, :], g_ref[...]).astype(BF16)
        for lo in range(0, d_ff, FFN_COL_TILE):
            cols = slice(lo, min(lo + FFN_COL_TILE, d_ff))
            a = _dot(h, wg_ref[:, cols])
            u = _dot(h, wu_ref[:, cols])
            act_ref[rows, cols] = (a * (1.0 / (1.0 + jnp.exp(-a))) * u).astype(BF16)
    for rows in halves:
        for lo in range(0, d, PROJ_COL_TILE):
            cols = slice(lo, lo + PROJ_COL_TILE)
            out_ref[0, rows, cols] = x1_ref[rows, cols] + _dot(act_ref[rows, :], wd_ref[:, cols])


def _post_call(x, mixes, w_out, g, wg, wu, wd):
    b, s, d = x.shape
    tm = min(TOKEN_TILE, s)
    n_mix = len(mixes)
    tok = lambda width: pl.BlockSpec((1, tm, width), lambda i, j: (i, j, 0))
    res = lambda shape: pl.BlockSpec(shape, lambda i, j: (0, 0), pipeline_mode=pl.Buffered(1))
    return pl.pallas_call(
        functools.partial(_post_kernel, n_mix=n_mix), grid=(b, s // tm),
        in_specs=([tok(d)] + [tok(m.shape[2]) for m in mixes]
                  + [res(w_out.shape), res((1, d)), res(wg.shape), res(wu.shape), res(wd.shape)]),
        out_specs=tok(d),
        out_shape=jax.ShapeDtypeStruct((b, s, d), x.dtype),
        scratch_shapes=[pltpu.VMEM((tm, d), F32), pltpu.VMEM((tm, wg.shape[1]), BF16)],
        compiler_params=_params(2), name="post")(x, *mixes, w_out, g, wg, wu, wd)


def _pair_gain(g):
    return jnp.concatenate([g, g]).astype(F32)[None, :]


def kernel(x, attn_norm_g, ffn_norm_g, even_w_in, fox_forget_b, fox_q_norm_g, fox_k_norm_g, even_w_out, diff_w_in, diff_q_norm_g, diff_k_norm_g, diff_lambda_q1, diff_lambda_k1, diff_lambda_q2, diff_lambda_k2, diff_subln_g, diff_w_out, rel_bias, ffn_w_gate, ffn_w_up, ffn_w_down):
    b, s, d = x.shape
    depth = attn_norm_g.shape[0]
    n_sb = d // (2 * HEAD_DIM)
    n_fox = d // (2 * HEAD_DIM)
    n_diff = d // (2 * HEAD_DIM)
    sb_w = n_sb * HEAD_DIM
    fox_w = n_fox * HEAD_DIM
    main_w = 3 * sb_w + 3 * fox_w
    assert sb_w % LANES == 0 and fox_w % LANES == 0 and n_fox <= LANES
    bias = _bias_tiles(rel_bias, _key_tile(s))

    for layer in range(depth):
        g_attn = attn_norm_g[layer][None, :]
        if layer % 2 == 0:
            e = layer // 2
            w_in = even_w_in[e]
            wfg = jnp.pad(w_in[:, main_w:], ((0, 0), (0, LANES - n_fox))).astype(BF16)
            fb = jnp.pad(fox_forget_b[e].astype(F32), (0, LANES - n_fox))[None, :]
            proj, c = _pre_call(x, g_attn, w_in[:, :main_w].astype(BF16), (wfg, fb))
            sbc, fxc = sb_w // LANES, fox_w // LANES
            o_a = _sb_call(proj, sbc, 0, sbc, 2 * sbc)
            o_b = _fox_call(proj, c, _pair_gain(fox_q_norm_g[e]), _pair_gain(fox_k_norm_g[e]),
                            fxc, 3 * sbc, 3 * sbc + fxc, 3 * sbc + 2 * fxc)
            mixes, w_out = [o_a, o_b], even_w_out[e].astype(BF16)
        else:
            o = layer // 2
            proj = _pre_call(x, g_attn, diff_w_in[o].astype(BF16))
            lam_rows = jnp.stack([diff_lambda_q1[o], diff_lambda_k1[o], diff_lambda_q2[o], diff_lambda_k2[o]])
            lam_rows = jnp.pad(lam_rows.astype(F32), ((0, 4), (0, LANES - HEAD_DIM)))
            lam_init = 0.8 - 0.6 * math.exp(-0.3 * layer)
            mix = _diff_call(proj, bias, _pair_gain(diff_q_norm_g[o]), _pair_gain(diff_k_norm_g[o]),
                             lam_rows, diff_subln_g[o].astype(F32)[None, :], n_diff, lam_init)
            mixes, w_out = [mix], diff_w_out[o].astype(BF16)
        x = _post_call(x, mixes, w_out, ffn_norm_g[layer][None, :], ffn_w_gate[layer].astype(BF16),
                       ffn_w_up[layer].astype(BF16), ffn_w_down[layer].astype(BF16))
    return x
```

```python
import functools
import math

import jax
import jax.numpy as jnp
import numpy as np
from jax import lax
from jax.experimental import pallas as pl
from jax.experimental.pallas import tpu as pltpu

F32 = jnp.float32
BF16 = jnp.bfloat16

HEAD_DIM = 64
N_BUCKETS = 32
MAX_DISTANCE = 128
RMS_EPS = 1e-6
LOG2E = 1.4426950408889634
NEG_BIG = -1e30
SIGN_BIT = np.uint32(0x80000000)

LANES = 128
K_TILE = 256
TOKEN_TILE = 512
PROJ_COL_TILE = 512
FFN_COL_TILE = 512
ROW_SPLIT = 2
ONES_ROWS = 16
SB_K_TILE = 128
SB_NEAR_TILES = 3
SB_ZERO_BITS = 152.0
SAFE_LOGIT_BITS = 30.0
QK_BOUND_SLACK = 1.01
VMEM_LIMIT = 56 * 1024 * 1024


def _dot(a, b):
    return jnp.dot(a, b, preferred_element_type=F32)


def _params(n_axes, vmem=VMEM_LIMIT):
    return pltpu.CompilerParams(dimension_semantics=("arbitrary",) * n_axes, vmem_limit_bytes=vmem)


def _rms_rows(x, g):
    ms = jnp.mean(x * x, axis=-1, keepdims=True)
    return x * lax.rsqrt(ms + RMS_EPS) * g


def _split3(x):
    hi = x.astype(BF16)
    r = x - hi.astype(F32)
    mid = r.astype(BF16)
    lo = (r - mid.astype(F32)).astype(BF16)
    return hi, mid, lo


def _row_groups(tm):
    step = tm // ROW_SPLIT
    return [slice(r, r + step) for r in range(0, tm, step)]


def _project(h, rows, w_ref, o_ref, col_tiles):
    for j in col_tiles:
        cols = slice(j * PROJ_COL_TILE, (j + 1) * PROJ_COL_TILE)
        o_ref[0, rows, cols] = _dot(h, w_ref[:, cols]).astype(o_ref.dtype)


def _pre_odd_kernel(x_ref, g_ref, w_ref, o_ref):
    groups = _row_groups(x_ref.shape[1])
    hs = [_rms_rows(x_ref[0, rows, :], g_ref[...]).astype(BF16) for rows in groups]
    for rows, h in zip(groups, hs):
        _project(h, rows, w_ref, o_ref, range(w_ref.shape[1] // PROJ_COL_TILE))


def _pre_even_kernel(x_ref, g_ref, w_ref, wfg_ref, fb_ref, ltri_ref, o_ref, c_ref, carry_ref):
    @pl.when(pl.program_id(1) == 0)
    def _():
        carry_ref[...] = jnp.zeros_like(carry_ref)

    groups = _row_groups(x_ref.shape[1])
    n_tiles = w_ref.shape[1] // PROJ_COL_TILE
    tc = ltri_ref.shape[0]
    assert groups[0].stop % tc == 0
    hs = [_rms_rows(x_ref[0, rows, :], g_ref[...]).astype(BF16) for rows in groups]
    pieces = []
    for rows, h in zip(groups, hs):
        fg = _dot(h, wfg_ref[...]) + fb_ref[...]
        logf = jnp.minimum(fg, 0.0) - jnp.log1p(jnp.exp(-jnp.abs(fg)))
        pieces.append([_split3(logf[r:r + tc]) for r in range(0, logf.shape[0], tc)])
    for rows, h in zip(groups, hs):
        _project(h, rows, w_ref, o_ref, range(n_tiles // 2))
    carry = carry_ref[...]
    for rows, chunks in zip(groups, pieces):
        for ci, (hi, mid, lo) in enumerate(chunks):
            cs = _dot(ltri_ref[...], hi) + _dot(ltri_ref[...], mid) + _dot(ltri_ref[...], lo) + carry
            c_ref[0, rows.start + ci * tc:rows.start + (ci + 1) * tc, :] = cs
            carry = cs[tc - 1:tc, :]
    carry_ref[...] = carry
    for rows, h in zip(groups, hs):
        _project(h, rows, w_ref, o_ref, range(n_tiles // 2, n_tiles))


def _pre_call(x, g, w, extra=None):
    b, s, d = x.shape
    n = w.shape[1]
    tm = min(TOKEN_TILE, s)
    grid = (b, s // tm)
    x_spec = pl.BlockSpec((1, tm, d), lambda i, j: (i, j, 0))
    g_spec = pl.BlockSpec((1, d), lambda i, j: (0, 0))
    w_spec = pl.BlockSpec((d, n), lambda i, j: (0, 0))
    o_spec = pl.BlockSpec((1, tm, n), lambda i, j: (i, j, 0))
    o_shape = jax.ShapeDtypeStruct((b, s, n), BF16)
    if extra is None:
        return pl.pallas_call(
            _pre_odd_kernel, grid=grid, in_specs=[x_spec, g_spec, w_spec], out_specs=o_spec,
            out_shape=o_shape, compiler_params=_params(2), name="pre_odd")(x, g, w)
    wfg, fb = extra
    tc = min(K_TILE, tm)
    ltri = (lax.broadcasted_iota(jnp.int32, (tc, tc), 0) >= lax.broadcasted_iota(jnp.int32, (tc, tc), 1)).astype(BF16)
    return pl.pallas_call(
        _pre_even_kernel, grid=grid,
        in_specs=[x_spec, g_spec, w_spec,
                  pl.BlockSpec((d, LANES), lambda i, j: (0, 0)),
                  pl.BlockSpec((1, LANES), lambda i, j: (0, 0)),
                  pl.BlockSpec((tc, tc), lambda i, j: (0, 0))],
        out_specs=[o_spec, pl.BlockSpec((1, tm, LANES), lambda i, j: (i, j, 0))],
        out_shape=[o_shape, jax.ShapeDtypeStruct((b, s, LANES), F32)],
        scratch_shapes=[pltpu.VMEM((1, LANES), F32)],
        compiler_params=_params(2), name="pre_even")(x, g, w, wfg, fb, ltri)


def _key_tile(s):
    tk = min(K_TILE, s)
    assert s % tk == 0 and tk % LANES == 0
    return tk


def _pipeline(items, stages, lag=1):
    carried = [{} for _ in stages]
    for step in range(len(items) + lag * (len(stages) - 1)):
        for i, stage in enumerate(stages):
            idx = step - i * lag
            if 0 <= idx < len(items):
                arg = carried[i - 1].pop(idx) if i else None
                carried[i][idx] = stage(items[idx], arg)


def _descending(n):
    return list(reversed(range(n)))


def _store_split_qt(qn, qat_ref, qbt_ref, col0):
    row = lax.broadcasted_iota(jnp.int32, (LANES, LANES), 0)
    for c in range(qn.shape[0] // LANES):
        cols = slice(col0 + c * LANES, col0 + (c + 1) * LANES)
        qt = qn[c * LANES:(c + 1) * LANES].T
        qat_ref[:, cols] = jnp.where(row < HEAD_DIM, qt, 0.0).astype(BF16)
        qbt_ref[:, cols] = jnp.where(row >= HEAD_DIM, qt, 0.0).astype(BF16)


def _group_rms(x, bd, g):
    ss = _dot((x * x).astype(BF16), bd)
    return x * lax.rsqrt(ss * (1.0 / HEAD_DIM) + RMS_EPS) * g


def _softmax_probs(d, zs, m, tk):
    out = []
    for cb in range(zs[0].shape[1] // tk):
        a = d + cb
        for i, z in enumerate(zs):
            zb = z[:, cb * tk:(cb + 1) * tk]
            m_cur = jnp.max(zb, axis=0, keepdims=True)
            if a == d:
                out.append((a, i, jnp.exp2(zb - m_cur).astype(BF16), None))
                m[i][a] = m_cur
            else:
                m_new = jnp.maximum(m[i][a], m_cur)
                out.append((a, i, jnp.exp2(zb - m_new).astype(BF16), jnp.exp2(m[i][a] - m_new)))
                m[i][a] = m_new
    return out


def _accumulate(probs, acc_refs, vts, tk):
    for a, i, p, alpha in probs:
        cols = slice(a * tk, (a + 1) * tk)
        if alpha is None:
            acc_refs[i][:, cols] = _dot(vts[i], p)
        else:
            acc_refs[i][:, cols] = acc_refs[i][:, cols] * alpha + _dot(vts[i], p)


def _with_ones(vt):
    return jnp.concatenate([vt.astype(BF16), jnp.ones((ONES_ROWS, vt.shape[1]), BF16)], axis=0)


def _sb_kernel(q_ref, k_ref, v_ref, u_ref, o_ref, qat_ref, qbt_ref, acc_ref, r_ref):
    s = q_ref.shape[1]
    tk = u_ref.shape[0]
    nblk = s // tk
    strict = lax.broadcasted_iota(jnp.int32, (tk, tk), 0) < lax.broadcasted_iota(jnp.int32, (tk, tk), 1)
    r = [[None] * nblk for _ in range(2)]

    def suffix_sum(z, tri):
        neg_abs = lax.bitcast_convert_type(lax.bitcast_convert_type(z, jnp.uint32) | SIGN_BIT, F32)
        sp = jnp.maximum(z, 0.0) + LOG2E * jnp.log(1.0 + jnp.exp2(neg_abs))
        if tri:
            sp = jnp.where(strict, sp, 0.0)
        cum = _dot(u_ref[...], sp.astype(BF16))
        return z - sp, cum, cum[0:1, :] + sp[0:1, :]

    def queries(d, _):
        qs = q_ref[0, d * tk:(d + 1) * tk, :].astype(F32) * (HEAD_DIM ** -0.5 * LOG2E)
        _store_split_qt(qs, qat_ref, qbt_ref, d * tk)

    def logits(d, _):
        kblk = k_ref[0, d * tk:(d + 1) * tk, :]
        cols = slice(d * tk, min(d + SB_NEAR_TILES, nblk) * tk)
        return [_dot(kblk, qt_ref[:, cols]) for qt_ref in (qat_ref, qbt_ref)]

    def suffix_sums(d, zs):
        chains = []
        for cb in range(zs[0].shape[1] // tk):
            for sub in range(2):
                chains.append((cb, sub) + suffix_sum(zs[sub][:, cb * tk:(cb + 1) * tk], cb == 0))
        return chains

    def values(d, chains):
        vt = v_ref[0, d * tk:(d + 1) * tk, :].astype(F32).T.astype(BF16)
        for cb, sub, log_beta, cum, total in chains:
            a = d + cb
            rows = slice(sub * HEAD_DIM, (sub + 1) * HEAD_DIM)
            cols = slice(a * tk, (a + 1) * tk)
            if cb == 0:
                w = jnp.where(strict, jnp.exp2(log_beta - cum), 0.0)
                acc_ref[rows, cols] = _dot(vt[rows], w.astype(BF16))
                r[sub][a] = total
            else:
                w = jnp.exp2(log_beta - cum - r[sub][a])
                acc_ref[rows, cols] += _dot(vt[rows], w.astype(BF16))
                r[sub][a] = r[sub][a] + total

    _pipeline(_descending(nblk), (queries, logits, suffix_sums, values))

    def far_tile(d):
        kblk = k_ref[0, d * tk:(d + 1) * tk, :]
        vt = v_ref[0, d * tk:(d + 1) * tk, :].astype(F32).T.astype(BF16)
        for a in range(d + SB_NEAR_TILES, nblk):
            cols = slice(a * tk, (a + 1) * tk)
            for sub, qt_ref in enumerate((qat_ref, qbt_ref)):
                rows = slice(sub * HEAD_DIM, (sub + 1) * HEAD_DIM)
                log_beta, cum, total = suffix_sum(_dot(kblk, qt_ref[:, cols]), False)
                r_old = r_ref[sub:sub + 1, cols]
                acc_ref[rows, cols] += _dot(vt[rows], jnp.exp2(log_beta - cum - r_old).astype(BF16))
                r_ref[sub:sub + 1, cols] = r_old + total

    if nblk > SB_NEAR_TILES:
        for sub in range(2):
            r_ref[sub:sub + 1, :] = jnp.concatenate(r[sub], axis=1)
        @pl.when(jnp.min(r_ref[0:2, SB_NEAR_TILES * tk:]) < SB_ZERO_BITS)
        def _():
            for d in reversed(range(nblk - SB_NEAR_TILES)):
                @pl.when(jnp.min(r_ref[0:2, (d + SB_NEAR_TILES) * tk:]) < SB_ZERO_BITS)
                def _(d=d):
                    far_tile(d)

    for c in range(s // LANES):
        cols = slice(c * LANES, (c + 1) * LANES)
        o_ref[0, cols, :] = acc_ref[:, cols].T.astype(o_ref.dtype)


def _sb_call(proj, n_pairs, q_col, k_col, v_col):
    b, s, _ = proj.shape
    tk = min(SB_K_TILE, s)
    assert s % tk == 0 and tk % LANES == 0
    u = (lax.broadcasted_iota(jnp.int32, (tk, tk), 1) > lax.broadcasted_iota(jnp.int32, (tk, tk), 0)).astype(BF16)
    tok = lambda col: pl.BlockSpec((1, s, LANES), lambda i, p: (i, 0, col + p))
    return pl.pallas_call(
        _sb_kernel, grid=(b, n_pairs),
        in_specs=[tok(q_col), tok(k_col), tok(v_col), pl.BlockSpec((tk, tk), lambda i, p: (0, 0))],
        out_specs=tok(0),
        out_shape=jax.ShapeDtypeStruct((b, s, n_pairs * LANES), BF16),
        scratch_shapes=[pltpu.VMEM((LANES, s), BF16), pltpu.VMEM((LANES, s), BF16), pltpu.VMEM((LANES, s), F32),
                        pltpu.VMEM((8, s), F32)],
        compiler_params=_params(2), name="sb_attn")(proj, proj, proj, u)


def _qk_logit_bound(gq_ref, gk_ref):
    gains = jnp.max(jnp.abs(gq_ref[...])) * jnp.max(jnp.abs(gk_ref[...]))
    return gains * (HEAD_DIM ** 0.5 * LOG2E * QK_BOUND_SLACK)


def _fox_kernel(q_ref, k_ref, v_ref, c_ref, gq_ref, gk_ref, bd_ref, o_ref,
                qat_ref, qbt_ref, acca_ref, accb_ref, ct_ref, kn_ref, vta_ref, vtb_ref, cb_ref):
    s = q_ref.shape[1]
    tk = _key_tile(s)
    nblk = s // tk
    p_idx = pl.program_id(1)
    sel_r = lax.broadcasted_iota(jnp.int32, (LANES, 2 * LANES), 0)
    sel_l = lax.broadcasted_iota(jnp.int32, (LANES, 2 * LANES), 1)
    sel = (sel_r == 2 * p_idx + (sel_l >= LANES).astype(jnp.int32)).astype(BF16)
    causal = lax.broadcasted_iota(jnp.int32, (tk, tk), 0) <= lax.broadcasted_iota(jnp.int32, (tk, tk), 1)
    head_row = lax.broadcasted_iota(jnp.int32, (LANES, LANES), 0)
    qt_refs, acc_refs = (qat_ref, qbt_ref), (acca_ref, accb_ref)

    def prepare_queries(a):
        qn = _group_rms(q_ref[0, a * tk:(a + 1) * tk, :].astype(F32), bd_ref[...], gq_ref[...])
        _store_split_qt(qn * (HEAD_DIM ** -0.5 * LOG2E), qat_ref, qbt_ref, a * tk)

    def normed_keys(d):
        return _group_rms(k_ref[0, d * tk:(d + 1) * tk, :].astype(F32), bd_ref[...], gk_ref[...]).astype(BF16)

    def key_decay(d):
        hi, mid, lo = _split3(c_ref[0, d * tk:(d + 1) * tk, :] * LOG2E)
        return _dot(hi, sel) + _dot(mid, sel) + _dot(lo, sel)

    def values_t(d):
        vt = v_ref[0, d * tk:(d + 1) * tk, :].astype(F32).T
        return _with_ones(vt[:HEAD_DIM]), _with_ones(vt[HEAD_DIM:])

    def finalize(a):
        for c in range(a * tk // LANES, (a + 1) * tk // LANES):
            cols = slice(c * LANES, (c + 1) * LANES)
            oa = acca_ref[0:HEAD_DIM, cols] / acca_ref[HEAD_DIM:HEAD_DIM + 1, cols]
            ob = accb_ref[0:HEAD_DIM, cols] / accb_ref[HEAD_DIM:HEAD_DIM + 1, cols]
            o_ref[0, cols, :] = jnp.concatenate([oa, ob], axis=0).T.astype(o_ref.dtype)

    def bounded_sweep():
        def prepare(item, _):
            a, d = item
            if d == a:
                prepare_queries(a)
                for c in range(a * tk // LANES, (a + 1) * tk // LANES):
                    cols = slice(c * LANES, (c + 1) * LANES)
                    ct = (c_ref[0, cols, :] * LOG2E).T
                    for sub in range(2):
                        ct_ref[sub:sub + 1, cols] = jnp.sum(jnp.where(head_row == 2 * p_idx + sub, ct, 0.0),
                                                            axis=0, keepdims=True)
                kn_ref[a * tk:(a + 1) * tk, :] = normed_keys(a)
                cb_ref[a] = key_decay(a)
                vta_ref[a], vtb_ref[a] = values_t(a)

        def logits(item, _):
            a, d = item
            cols = slice(a * tk, (a + 1) * tk)
            kn = kn_ref[d * tk:(d + 1) * tk, :]
            zs = []
            for sub, qt_ref in enumerate(qt_refs):
                decay = cb_ref[d, :, sub * LANES:(sub + 1) * LANES]
                z = (_dot(kn, qt_ref[:, cols]) - jnp.concatenate([decay] * (tk // LANES), axis=1)
                     + ct_ref[sub:sub + 1, cols])
                zs.append(jnp.where(causal, z, NEG_BIG) if d == a else z)
            return zs

        def probs(item, zs):
            return [jnp.exp2(z).astype(BF16) for z in zs]

        def values(item, ps):
            a, d = item
            cols = slice(a * tk, (a + 1) * tk)
            for acc_ref, vt_ref, p in zip(acc_refs, (vta_ref, vtb_ref), ps):
                if d == a:
                    acc_ref[:, cols] = _dot(vt_ref[d], p)
                else:
                    acc_ref[:, cols] += _dot(vt_ref[d], p)
            if d == 0:
                finalize(a)

        items = [(a, d) for a in range(nblk) for d in _descending(a + 1)]
        _pipeline(items, (prepare, logits, probs, values))

    def running_max_sweep():
        m = [[None] * nblk for _ in range(2)]

        def queries(d, _):
            prepare_queries(d)

        def logits(d, _):
            kn = normed_keys(d)
            cb = key_decay(d)
            zs = []
            for sub, qt_ref in enumerate(qt_refs):
                decay = cb[:, sub * LANES:(sub + 1) * LANES]
                z = _dot(kn, qt_ref[:, d * tk:]) - jnp.concatenate([decay] * ((s - d * tk) // LANES), axis=1)
                tri = jnp.where(causal, z[:, :tk], NEG_BIG)
                zs.append(tri if d == nblk - 1 else jnp.concatenate([tri, z[:, tk:]], axis=1))
            return zs

        def probs(d, zs):
            return _softmax_probs(d, zs, m, tk)

        def values(d, ps):
            _accumulate(ps, acc_refs, values_t(d), tk)

        _pipeline(_descending(nblk), (queries, logits, probs, values))
        for a in range(nblk):
            finalize(a)

    is_bounded = _qk_logit_bound(gq_ref, gk_ref) <= SAFE_LOGIT_BITS
    pl.when(is_bounded)(bounded_sweep)
    pl.when(jnp.logical_not(is_bounded))(running_max_sweep)


def _block_diag_ones():
    blk = lax.broadcasted_iota(jnp.int32, (LANES, LANES), 0) // HEAD_DIM
    return (blk == lax.broadcasted_iota(jnp.int32, (LANES, LANES), 1) // HEAD_DIM).astype(BF16)


def _fox_call(proj, c, gq, gk, n_pairs, q_col, k_col, v_col):
    b, s, _ = proj.shape
    tk = _key_tile(s)
    tok = lambda col: pl.BlockSpec((1, s, LANES), lambda i, p: (i, 0, col + p))
    const = lambda shape: pl.BlockSpec(shape, lambda i, p: (0,) * len(shape))
    rows_aug = HEAD_DIM + ONES_ROWS
    return pl.pallas_call(
        _fox_kernel, grid=(b, n_pairs),
        in_specs=[tok(q_col), tok(k_col), tok(v_col), pl.BlockSpec((1, s, LANES), lambda i, p: (i, 0, 0)),
                  const((1, LANES)), const((1, LANES)), const((LANES, LANES))],
        out_specs=tok(0),
        out_shape=jax.ShapeDtypeStruct((b, s, n_pairs * LANES), BF16),
        scratch_shapes=[pltpu.VMEM((LANES, s), BF16), pltpu.VMEM((LANES, s), BF16),
                        pltpu.VMEM((rows_aug, s), F32), pltpu.VMEM((rows_aug, s), F32), pltpu.VMEM((8, s), F32),
                        pltpu.VMEM((s, LANES), BF16),
                        pltpu.VMEM((s // tk, rows_aug, tk), BF16), pltpu.VMEM((s // tk, rows_aug, tk), BF16),
                        pltpu.VMEM((s // tk, tk, 2 * LANES), F32)],
        compiler_params=_params(2), name="fox_attn")(proj, proj, proj, c, gq, gk, _block_diag_ones())


def _diff_kernel(q_ref, k_ref, v_ref, bias_ref, gq_ref, gk_ref, bd_ref, lam_ref, sg_ref, o_ref,
                 q1t_ref, q2t_ref, acc1_ref, acc2_ref, kn_ref, vt_ref, *, lam_init):
    s = q_ref.shape[1]
    tk = bias_ref.shape[2]
    nblk = s // tk
    dv = v_ref.shape[2]
    causal = lax.broadcasted_iota(jnp.int32, (tk, tk), 0) <= lax.broadcasted_iota(jnp.int32, (tk, tk), 1)
    lam = (jnp.exp(jnp.sum(lam_ref[0:1, :] * lam_ref[1:2, :], axis=-1, keepdims=True))
           - jnp.exp(jnp.sum(lam_ref[2:3, :] * lam_ref[3:4, :], axis=-1, keepdims=True)) + lam_init)
    out_gain = sg_ref[...] * (1.0 - lam_init)

    def prepare_queries(a):
        qn = _group_rms(q_ref[0, a * tk:(a + 1) * tk, :].astype(F32), bd_ref[...], gq_ref[...])
        _store_split_qt(qn * (HEAD_DIM ** -0.5 * LOG2E), q1t_ref, q2t_ref, a * tk)

    def normed_keys(d):
        return _group_rms(k_ref[0, d * tk:(d + 1) * tk, :].astype(F32), bd_ref[...], gk_ref[...]).astype(BF16)

    def values_t(d):
        return _with_ones(v_ref[0, d * tk:(d + 1) * tk, :].astype(F32).T)

    def biased(z, a, d):
        if d == a:
            return jnp.where(causal, z + bias_ref[0, 0], NEG_BIG)
        return z + bias_ref[0, 1] if d == a - 1 else z

    def finalize(a):
        for c in range(a * tk // LANES, (a + 1) * tk // LANES):
            cols = slice(c * LANES, (c + 1) * LANES)
            o1 = acc1_ref[0:dv, cols] / acc1_ref[dv:dv + 1, cols]
            o2 = acc2_ref[0:dv, cols] / acc2_ref[dv:dv + 1, cols]
            o = o1 - lam * o2
            o = o * lax.rsqrt(jnp.mean(o * o, axis=0, keepdims=True) + RMS_EPS)
            o_ref[0, cols, :] = (o.T * out_gain).astype(o_ref.dtype)

    def bounded_sweep():
        def prepare(item, _):
            a, d = item
            if d == a:
                prepare_queries(a)
                kn_ref[a * tk:(a + 1) * tk, :] = normed_keys(a)
                vt_ref[a] = values_t(a)

        def logits(item, _):
            a, d = item
            kn = kn_ref[d * tk:(d + 1) * tk, :]
            return [biased(_dot(kn, qt_ref[:, a * tk:(a + 1) * tk]), a, d) for qt_ref in (q1t_ref, q2t_ref)]

        def probs(item, zs):
            return [jnp.exp2(z).astype(BF16) for z in zs]

        def values(item, ps):
            a, d = item
            cols = slice(a * tk, (a + 1) * tk)
            for acc_ref, p in zip((acc1_ref, acc2_ref), ps):
                if d == a:
                    acc_ref[:, cols] = _dot(vt_ref[d], p)
                else:
                    acc_ref[:, cols] += _dot(vt_ref[d], p)
            if d == 0:
                finalize(a)

        items = [(a, d) for a in range(nblk) for d in _descending(a + 1)]
        _pipeline(items, (prepare, logits, probs, values), lag=3)

    def running_max_sweep():
        m = [[None] * nblk for _ in range(2)]

        def queries(d, _):
            prepare_queries(d)

        def logits(d, _):
            kn = normed_keys(d)
            zs = []
            for qt_ref in (q1t_ref, q2t_ref):
                z = _dot(kn, qt_ref[:, d * tk:])
                zs.append(jnp.concatenate([biased(z[:, (a - d) * tk:(a - d + 1) * tk], a, d) for a in range(d, min(d + 2, nblk))]
                                          + ([z[:, 2 * tk:]] if d < nblk - 2 else []), axis=1))
            return zs

        def probs(d, zs):
            return _softmax_probs(d, zs, m, tk)

        def values(d, ps):
            vt = values_t(d)
            _accumulate(ps, (acc1_ref, acc2_ref), (vt, vt), tk)

        _pipeline(_descending(nblk), (queries, logits, probs, values))
        for a in range(nblk):
            finalize(a)

    is_bounded = _qk_logit_bound(gq_ref, gk_ref) + jnp.max(jnp.abs(bias_ref[...])) <= SAFE_LOGIT_BITS
    pl.when(is_bounded)(bounded_sweep)
    pl.when(jnp.logical_not(is_bounded))(running_max_sweep)


def _diff_call(proj, bias, gq, gk, lam_rows, subln_g, n_heads, lam_init):
    b, s, _ = proj.shape
    tk = _key_tile(s)
    tok = lambda col: pl.BlockSpec((1, s, LANES), lambda i, p: (i, 0, col + p))
    const = lambda shape: pl.BlockSpec(shape, lambda i, p: (0,) * len(shape))
    dv = 2 * HEAD_DIM
    assert dv == LANES
    rows_aug = dv + ONES_ROWS
    return pl.pallas_call(
        functools.partial(_diff_kernel, lam_init=lam_init), grid=(b, n_heads),
        in_specs=[tok(0), tok(n_heads), tok(2 * n_heads),
                  pl.BlockSpec((1, 2, tk, tk), lambda i, p: (p, 0, 0, 0)),
                  const((1, LANES)), const((1, LANES)), const((LANES, LANES)), const((8, LANES)), const((1, dv))],
        out_specs=tok(0),
        out_shape=jax.ShapeDtypeStruct((b, s, n_heads * dv), BF16),
        scratch_shapes=[pltpu.VMEM((LANES, s), BF16), pltpu.VMEM((LANES, s), BF16),
                        pltpu.VMEM((rows_aug, s), F32), pltpu.VMEM((rows_aug, s), F32),
                        pltpu.VMEM((s, LANES), BF16), pltpu.VMEM((s // tk, rows_aug, tk), BF16)],
        compiler_params=_params(2), name="diff_attn")(proj, proj, proj, bias, gq, gk, _block_diag_ones(),
                                                      lam_rows, subln_g)


def _t5_bucket(dist):
    max_exact = N_BUCKETS // 2
    nf = jnp.maximum(dist, 1).astype(F32)
    large = max_exact + (jnp.log(nf / max_exact) / math.log(MAX_DISTANCE / max_exact)
                         * (N_BUCKETS - max_exact)).astype(jnp.int32)
    large = jnp.minimum(large, N_BUCKETS - 1)
    return jnp.where(dist < max_exact, dist, large)


def _bias_tiles(rel_bias, t):
    assert t >= MAX_DISTANCE
    table = rel_bias.astype(F32)
    krow = lax.broadcasted_iota(jnp.int32, (2, t, t), 1)
    qcol = lax.broadcasted_iota(jnp.int32, (2, t, t), 2)
    dist = qcol - krow + t * lax.broadcasted_iota(jnp.int32, (2, t, t), 0)
    onehot = (_t5_bucket(jnp.maximum(dist, 0))[..., None] == jnp.arange(N_BUCKETS)).astype(F32)
    return jnp.einsum('otkb,bh->hotk', onehot, (table - table[N_BUCKETS - 1]) * LOG2E,
                      precision=lax.Precision.HIGHEST)


def _post_kernel(*refs, n_mix):
    x_ref = refs[0]
    o_refs = refs[1:1 + n_mix]
    wo_ref, g_ref, wg_ref, wu_ref, wd_ref, out_ref, x1_ref, act_ref = refs[1 + n_mix:]
    tm, d = x_ref.shape[1:]
    d_ff = wg_ref.shape[1]
    halves = _row_groups(tm)
    for rows in halves:
        o = jnp.concatenate([o_ref[0, rows, :] for o_ref in o_refs], axis=1) if n_mix > 1 else o_refs[0][0, rows, :]
        for lo in range(0, d, PROJ_COL_TILE):
            cols = slice(lo, lo + PROJ_COL_TILE)
            x1_ref[rows, cols] = x_ref[0, rows, cols] + _dot(o, wo_ref[:, cols])
    for rows in halves:
        h = _rms_rows(x1_ref[rows, :], g_ref[...]).astype(BF16)
        for lo in range(0, d_ff, FFN_COL_TILE):
            cols = slice(lo, min(lo + FFN_COL_TILE, d_ff))
            a = _dot(h, wg_ref[:, cols])
            u = _dot(h, wu_ref[:, cols])
            act_ref[rows, cols] = (a * (1.0 / (1.0 + jnp.exp(-a))) * u).astype(BF16)
    for rows in halves:
        for lo in range(0, d, PROJ_COL_TILE):
            cols = slice(lo, lo + PROJ_COL_TILE)
            out_ref[0, rows, cols] = x1_ref[rows, cols] + _dot(act_ref[rows, :], wd_ref[:, cols])


def _post_call(x, mixes, w_out, g, wg, wu, wd):
    b, s, d = x.shape
    tm = min(TOKEN_TILE, s)
    n_mix = len(mixes)
    tok = lambda width: pl.BlockSpec((1, tm, width), lambda i, j: (i, j, 0))
    res = lambda shape: pl.BlockSpec(shape, lambda i, j: (0, 0), pipeline_mode=pl.Buffered(1))
    return pl.pallas_call(
        functools.partial(_post_kernel, n_mix=n_mix), grid=(b, s // tm),
        in_specs=([tok(d)] + [tok(m.shape[2]) for m in mixes]
                  + [res(w_out.shape), res((1, d)), res(wg.shape), res(wu.shape), res(wd.shape)]),
        out_specs=tok(d),
        out_shape=jax.ShapeDtypeStruct((b, s, d), x.dtype),
        scratch_shapes=[pltpu.VMEM((tm, d), F32), pltpu.VMEM((tm, wg.shape[1]), BF16)],
        compiler_params=_params(2), name="post")(x, *mixes, w_out, g, wg, wu, wd)


def _pair_gain(g):
    return jnp.concatenate([g, g]).astype(F32)[None, :]


def kernel(x, attn_norm_g, ffn_norm_g, even_w_in, fox_forget_b, fox_q_norm_g, fox_k_norm_g, even_w_out, diff_w_in, diff_q_norm_g, diff_k_norm_g, diff_lambda_q1, diff_lambda_k1, diff_lambda_q2, diff_lambda_k2, diff_subln_g, diff_w_out, rel_bias, ffn_w_gate, ffn_w_up, ffn_w_down):
    b, s, d = x.shape
    depth = attn_norm_g.shape[0]
    n_sb = d // (2 * HEAD_DIM)
    n_fox = d // (2 * HEAD_DIM)
    n_diff = d // (2 * HEAD_DIM)
    sb_w = n_sb * HEAD_DIM
    fox_w = n_fox * HEAD_DIM
    main_w = 3 * sb_w + 3 * fox_w
    assert sb_w % LANES == 0 and fox_w % LANES == 0 and n_fox <= LANES
    bias = _bias_tiles(rel_bias, _key_tile(s))

    for layer in range(depth):
        g_attn = attn_norm_g[layer][None, :]
        if layer % 2 == 0:
            e = layer // 2
            w_in = even_w_in[e]
            wfg = jnp.pad(w_in[:, main_w:], ((0, 0), (0, LANES - n_fox))).astype(BF16)
            fb = jnp.pad(fox_forget_b[e].astype(F32), (0, LANES - n_fox))[None, :]
            proj, c = _pre_call(x, g_attn, w_in[:, :main_w].astype(BF16), (wfg, fb))
            sbc, fxc = sb_w // LANES, fox_w // LANES
            o_a = _sb_call(proj, sbc, 0, sbc, 2 * sbc)
            o_b = _fox_call(proj, c, _pair_gain(fox_q_norm_g[e]), _pair_gain(fox_k_norm_g[e]),
                            fxc, 3 * sbc, 3 * sbc + fxc, 3 * sbc + 2 * fxc)
            mixes, w_out = [o_a, o_b], even_w_out[e].astype(BF16)
        else:
            o = layer // 2
            proj = _pre_call(x, g_attn, diff_w_in[o].astype(BF16))
            lam_rows = jnp.stack([diff_lambda_q1[o], diff_lambda_k1[o], diff_lambda_q2[o], diff_lambda_k2[o]])
            lam_rows = jnp.pad(lam_rows.astype(F32), ((0, 4), (0, LANES - HEAD_DIM)))
            lam_init = 0.8 - 0.6 * math.exp(-0.3 * layer)
            mix = _diff_call(proj, bias, _pair_gain(diff_q_norm_g[o]), _pair_gain(diff_k_norm_g[o]),
                             lam_rows, diff_subln_g[o].astype(F32)[None, :], n_diff, lam_init)
            mixes, w_out = [mix], diff_w_out[o].astype(BF16)
        x = _post_call(x, mixes, w_out, ffn_norm_g[layer][None, :], ffn_w_gate[layer].astype(BF16),
                       ffn_w_up[layer].astype(BF16), ffn_w_down[layer].astype(BF16))
    return x
```

```python
import functools
import math

import jax
import jax.numpy as jnp
import numpy as np
from jax import lax
from jax.experimental import pallas as pl
from jax.experimental.pallas import tpu as pltpu

F32 = jnp.float32
BF16 = jnp.bfloat16

HEAD_DIM = 64
N_BUCKETS = 32
MAX_DISTANCE = 128
RMS_EPS = 1e-6
LOG2E = 1.4426950408889634
NEG_BIG = -1e30
SIGN_BIT = np.uint32(0x80000000)

LANES = 128
K_TILE = 256
TOKEN_TILE = 512
PROJ_COL_TILE = 512
FFN_COL_TILE = 512
ROW_SPLIT = 2
ONES_ROWS = 16
SB_K_TILE = 128
SB_NEAR_TILES = 3
SB_ZERO_BITS = 152.0
SAFE_LOGIT_BITS = 30.0
QK_BOUND_SLACK = 1.02
VMEM_LIMIT = 56 * 1024 * 1024


def _dot(a, b):
    return jnp.dot(a, b, preferred_element_type=F32)


def _params(n_axes, vmem=VMEM_LIMIT):
    return pltpu.CompilerParams(dimension_semantics=("arbitrary",) * n_axes, vmem_limit_bytes=vmem)


def _rms_rows(x, g):
    ms = jnp.mean(x * x, axis=-1, keepdims=True)
    return x * lax.rsqrt(ms + RMS_EPS) * g


def _split3(x):
    hi = x.astype(BF16)
    r = x - hi.astype(F32)
    mid = r.astype(BF16)
    lo = (r - mid.astype(F32)).astype(BF16)
    return hi, mid, lo


def _row_groups(tm):
    step = tm // ROW_SPLIT
    return [slice(r, r + step) for r in range(0, tm, step)]


def _project(h, rows, w_ref, o_ref, col_tiles):
    for j in col_tiles:
        cols = slice(j * PROJ_COL_TILE, (j + 1) * PROJ_COL_TILE)
        o_ref[0, rows, cols] = _dot(h, w_ref[:, cols]).astype(o_ref.dtype)


def _pre_odd_kernel(x_ref, g_ref, w_ref, o_ref):
    groups = _row_groups(x_ref.shape[1])
    hs = [_rms_rows(x_ref[0, rows, :], g_ref[...]).astype(BF16) for rows in groups]
    for rows, h in zip(groups, hs):
        _project(h, rows, w_ref, o_ref, range(w_ref.shape[1] // PROJ_COL_TILE))


def _pre_even_kernel(x_ref, g_ref, w_ref, wfg_ref, fb_ref, ltri_ref, o_ref, c_ref, carry_ref):
    @pl.when(pl.program_id(1) == 0)
    def _():
        carry_ref[...] = jnp.zeros_like(carry_ref)

    groups = _row_groups(x_ref.shape[1])
    n_tiles = w_ref.shape[1] // PROJ_COL_TILE
    tc = ltri_ref.shape[0]
    assert groups[0].stop % tc == 0
    hs = [_rms_rows(x_ref[0, rows, :], g_ref[...]).astype(BF16) for rows in groups]
    pieces = []
    for rows, h in zip(groups, hs):
        fg = _dot(h, wfg_ref[...]) + fb_ref[...]
        logf = jnp.minimum(fg, 0.0) - jnp.log1p(jnp.exp(-jnp.abs(fg)))
        pieces.append([_split3(logf[r:r + tc]) for r in range(0, logf.shape[0], tc)])
    for rows, h in zip(groups, hs):
        _project(h, rows, w_ref, o_ref, range(n_tiles // 2))
    carry = carry_ref[...]
    for rows, chunks in zip(groups, pieces):
        for ci, (hi, mid, lo) in enumerate(chunks):
            cs = _dot(ltri_ref[...], hi) + _dot(ltri_ref[...], mid) + _dot(ltri_ref[...], lo) + carry
            c_ref[0, rows.start + ci * tc:rows.start + (ci + 1) * tc, :] = cs
            carry = cs[tc - 1:tc, :]
    carry_ref[...] = carry
    for rows, h in zip(groups, hs):
        _project(h, rows, w_ref, o_ref, range(n_tiles // 2, n_tiles))


def _pre_call(x, g, w, extra=None):
    b, s, d = x.shape
    n = w.shape[1]
    tm = min(TOKEN_TILE, s)
    grid = (b, s // tm)
    x_spec = pl.BlockSpec((1, tm, d), lambda i, j: (i, j, 0))
    g_spec = pl.BlockSpec((1, d), lambda i, j: (0, 0))
    w_spec = pl.BlockSpec((d, n), lambda i, j: (0, 0))
    o_spec = pl.BlockSpec((1, tm, n), lambda i, j: (i, j, 0))
    o_shape = jax.ShapeDtypeStruct((b, s, n), BF16)
    if extra is None:
        return pl.pallas_call(
            _pre_odd_kernel, grid=grid, in_specs=[x_spec, g_spec, w_spec], out_specs=o_spec,
            out_shape=o_shape, compiler_params=_params(2), name="pre_odd")(x, g, w)
    wfg, fb = extra
    tc = min(K_TILE, tm)
    ltri = (lax.broadcasted_iota(jnp.int32, (tc, tc), 0) >= lax.broadcasted_iota(jnp.int32, (tc, tc), 1)).astype(BF16)
    return pl.pallas_call(
        _pre_even_kernel, grid=grid,
        in_specs=[x_spec, g_spec, w_spec,
                  pl.BlockSpec((d, LANES), lambda i, j: (0, 0)),
                  pl.BlockSpec((1, LANES), lambda i, j: (0, 0)),
                  pl.BlockSpec((tc, tc), lambda i, j: (0, 0))],
        out_specs=[o_spec, pl.BlockSpec((1, tm, LANES), lambda i, j: (i, j, 0))],
        out_shape=[o_shape, jax.ShapeDtypeStruct((b, s, LANES), F32)],
        scratch_shapes=[pltpu.VMEM((1, LANES), F32)],
        compiler_params=_params(2), name="pre_even")(x, g, w, wfg, fb, ltri)


def _key_tile(s):
    tk = min(K_TILE, s)
    assert s % tk == 0 and tk % LANES == 0
    return tk


def _pipeline(items, stages, lag=1):
    carried = [{} for _ in stages]
    for step in range(len(items) + lag * (len(stages) - 1)):
        for i, stage in enumerate(stages):
            idx = step - i * lag
            if 0 <= idx < len(items):
                arg = carried[i - 1].pop(idx) if i else None
                carried[i][idx] = stage(items[idx], arg)


def _descending(n):
    return list(reversed(range(n)))


def _store_split_qt(qn, qat_ref, qbt_ref, col0):
    row = lax.broadcasted_iota(jnp.int32, (LANES, LANES), 0)
    for c in range(qn.shape[0] // LANES):
        cols = slice(col0 + c * LANES, col0 + (c + 1) * LANES)
        qt = qn[c * LANES:(c + 1) * LANES].T
        qat_ref[:, cols] = jnp.where(row < HEAD_DIM, qt, 0.0).astype(BF16)
        qbt_ref[:, cols] = jnp.where(row >= HEAD_DIM, qt, 0.0).astype(BF16)


def _group_rms(x, bd, g):
    ss = _dot((x * x).astype(BF16), bd)
    return x * lax.rsqrt(ss * (1.0 / HEAD_DIM) + RMS_EPS) * g


def _softmax_probs(d, zs, m, tk):
    out = []
    for cb in range(zs[0].shape[1] // tk):
        a = d + cb
        for i, z in enumerate(zs):
            zb = z[:, cb * tk:(cb + 1) * tk]
            m_cur = jnp.max(zb, axis=0, keepdims=True)
            if a == d:
                out.append((a, i, jnp.exp2(zb - m_cur).astype(BF16), None))
                m[i][a] = m_cur
            else:
                m_new = jnp.maximum(m[i][a], m_cur)
                out.append((a, i, jnp.exp2(zb - m_new).astype(BF16), jnp.exp2(m[i][a] - m_new)))
                m[i][a] = m_new
    return out


def _accumulate(probs, acc_refs, vts, tk):
    for a, i, p, alpha in probs:
        cols = slice(a * tk, (a + 1) * tk)
        if alpha is None:
            acc_refs[i][:, cols] = _dot(vts[i], p)
        else:
            acc_refs[i][:, cols] = acc_refs[i][:, cols] * alpha + _dot(vts[i], p)


def _with_ones(vt):
    return jnp.concatenate([vt.astype(BF16), jnp.ones((ONES_ROWS, vt.shape[1]), BF16)], axis=0)


def _sb_kernel(q_ref, k_ref, v_ref, u_ref, o_ref, qat_ref, qbt_ref, acc_ref, r_ref):
    s = q_ref.shape[1]
    tk = u_ref.shape[0]
    nblk = s // tk
    strict = lax.broadcasted_iota(jnp.int32, (tk, tk), 0) < lax.broadcasted_iota(jnp.int32, (tk, tk), 1)
    r = [[None] * nblk for _ in range(2)]

    def suffix_sum(z, tri):
        neg_abs = lax.bitcast_convert_type(lax.bitcast_convert_type(z, jnp.uint32) | SIGN_BIT, F32)
        sp = jnp.maximum(z, 0.0) + LOG2E * jnp.log(1.0 + jnp.exp2(neg_abs))
        if tri:
            sp = jnp.where(strict, sp, 0.0)
        cum = _dot(u_ref[...], sp.astype(BF16))
        return z - sp, cum, cum[0:1, :] + sp[0:1, :]

    def queries(d, _):
        qs = q_ref[0, d * tk:(d + 1) * tk, :].astype(F32) * (HEAD_DIM ** -0.5 * LOG2E)
        _store_split_qt(qs, qat_ref, qbt_ref, d * tk)

    def logits(d, _):
        kblk = k_ref[0, d * tk:(d + 1) * tk, :]
        cols = slice(d * tk, min(d + SB_NEAR_TILES, nblk) * tk)
        return [_dot(kblk, qt_ref[:, cols]) for qt_ref in (qat_ref, qbt_ref)]

    def suffix_sums(d, zs):
        chains = []
        for cb in range(zs[0].shape[1] // tk):
            for sub in range(2):
                chains.append((cb, sub) + suffix_sum(zs[sub][:, cb * tk:(cb + 1) * tk], cb == 0))
        return chains

    def values(d, chains):
        vt = v_ref[0, d * tk:(d + 1) * tk, :].astype(F32).T.astype(BF16)
        for cb, sub, log_beta, cum, total in chains:
            a = d + cb
            rows = slice(sub * HEAD_DIM, (sub + 1) * HEAD_DIM)
            cols = slice(a * tk, (a + 1) * tk)
            if cb == 0:
                w = jnp.where(strict, jnp.exp2(log_beta - cum), 0.0)
                acc_ref[rows, cols] = _dot(vt[rows], w.astype(BF16))
                r[sub][a] = total
            else:
                w = jnp.exp2(log_beta - cum - r[sub][a])
                acc_ref[rows, cols] += _dot(vt[rows], w.astype(BF16))
                r[sub][a] = r[sub][a] + total

    _pipeline(_descending(nblk), (queries, logits, suffix_sums, values))

    def far_tile(d):
        kblk = k_ref[0, d * tk:(d + 1) * tk, :]
        vt = v_ref[0, d * tk:(d + 1) * tk, :].astype(F32).T.astype(BF16)
        for a in range(d + SB_NEAR_TILES, nblk):
            cols = slice(a * tk, (a + 1) * tk)
            for sub, qt_ref in enumerate((qat_ref, qbt_ref)):
                rows = slice(sub * HEAD_DIM, (sub + 1) * HEAD_DIM)
                log_beta, cum, total = suffix_sum(_dot(kblk, qt_ref[:, cols]), False)
                r_old = r_ref[sub:sub + 1, cols]
                acc_ref[rows, cols] += _dot(vt[rows], jnp.exp2(log_beta - cum - r_old).astype(BF16))
                r_ref[sub:sub + 1, cols] = r_old + total

    if nblk > SB_NEAR_TILES:
        for sub in range(2):
            r_ref[sub:sub + 1, :] = jnp.concatenate(r[sub], axis=1)
        @pl.when(jnp.min(r_ref[0:2, SB_NEAR_TILES * tk:]) < SB_ZERO_BITS)
        def _():
            for d in reversed(range(nblk - SB_NEAR_TILES)):
                @pl.when(jnp.min(r_ref[0:2, (d + SB_NEAR_TILES) * tk:]) < SB_ZERO_BITS)
                def _(d=d):
                    far_tile(d)

    for c in range(s // LANES):
        cols = slice(c * LANES, (c + 1) * LANES)
        o_ref[0, cols, :] = acc_ref[:, cols].T.astype(o_ref.dtype)


def _sb_call(proj, n_pairs, q_col, k_col, v_col):
    b, s, _ = proj.shape
    tk = min(SB_K_TILE, s)
    assert s % tk == 0 and tk % LANES == 0
    u = (lax.broadcasted_iota(jnp.int32, (tk, tk), 1) > lax.broadcasted_iota(jnp.int32, (tk, tk), 0)).astype(BF16)
    tok = lambda col: pl.BlockSpec((1, s, LANES), lambda i, p: (i, 0, col + p))
    return pl.pallas_call(
        _sb_kernel, grid=(b, n_pairs),
        in_specs=[tok(q_col), tok(k_col), tok(v_col), pl.BlockSpec((tk, tk), lambda i, p: (0, 0))],
        out_specs=tok(0),
        out_shape=jax.ShapeDtypeStruct((b, s, n_pairs * LANES), BF16),
        scratch_shapes=[pltpu.VMEM((LANES, s), BF16), pltpu.VMEM((LANES, s), BF16), pltpu.VMEM((LANES, s), F32),
                        pltpu.VMEM((8, s), F32)],
        compiler_params=_params(2), name="sb_attn")(proj, proj, proj, u)


def _bounded_flags(gq, gk, extra_bits):
    qk_bits = jnp.max(jnp.abs(gq)) * jnp.max(jnp.abs(gk)) * (HEAD_DIM ** 0.5 * LOG2E * QK_BOUND_SLACK)
    return (qk_bits + extra_bits <= SAFE_LOGIT_BITS).astype(jnp.int32)


def _fox_kernel(bounded_ref, q_ref, k_ref, v_ref, c_ref, gq_ref, gk_ref, bd_ref, o_ref,
                qat_ref, qbt_ref, acca_ref, accb_ref, ct_ref, kn_ref, vta_ref, vtb_ref, cb_ref):
    s = q_ref.shape[1]
    tk = _key_tile(s)
    nblk = s // tk
    p_idx = pl.program_id(1)
    sel_r = lax.broadcasted_iota(jnp.int32, (LANES, 2 * LANES), 0)
    sel_l = lax.broadcasted_iota(jnp.int32, (LANES, 2 * LANES), 1)
    sel = (sel_r == 2 * p_idx + (sel_l >= LANES).astype(jnp.int32)).astype(BF16)
    causal = lax.broadcasted_iota(jnp.int32, (tk, tk), 0) <= lax.broadcasted_iota(jnp.int32, (tk, tk), 1)
    head_row = lax.broadcasted_iota(jnp.int32, (LANES, LANES), 0)
    qt_refs, acc_refs = (qat_ref, qbt_ref), (acca_ref, accb_ref)

    def prepare_queries(a):
        qn = _group_rms(q_ref[0, a * tk:(a + 1) * tk, :].astype(F32), bd_ref[...], gq_ref[...])
        _store_split_qt(qn * (HEAD_DIM ** -0.5 * LOG2E), qat_ref, qbt_ref, a * tk)

    def normed_keys(d):
        return _group_rms(k_ref[0, d * tk:(d + 1) * tk, :].astype(F32), bd_ref[...], gk_ref[...]).astype(BF16)

    def key_decay(d):
        hi, mid, lo = _split3(c_ref[0, d * tk:(d + 1) * tk, :] * LOG2E)
        return _dot(hi, sel) + _dot(mid, sel) + _dot(lo, sel)

    def values_t(d):
        vt = v_ref[0, d * tk:(d + 1) * tk, :].astype(F32).T
        return _with_ones(vt[:HEAD_DIM]), _with_ones(vt[HEAD_DIM:])

    def finalize(a):
        for c in range(a * tk // LANES, (a + 1) * tk // LANES):
            cols = slice(c * LANES, (c + 1) * LANES)
            oa = acca_ref[0:HEAD_DIM, cols] / acca_ref[HEAD_DIM:HEAD_DIM + 1, cols]
            ob = accb_ref[0:HEAD_DIM, cols] / accb_ref[HEAD_DIM:HEAD_DIM + 1, cols]
            o_ref[0, cols, :] = jnp.concatenate([oa, ob], axis=0).T.astype(o_ref.dtype)

    def bounded_sweep():
        def prepare(item, _):
            a, d = item
            if d == a:
                prepare_queries(a)
                for c in range(a * tk // LANES, (a + 1) * tk // LANES):
                    cols = slice(c * LANES, (c + 1) * LANES)
                    ct = (c_ref[0, cols, :] * LOG2E).T
                    for sub in range(2):
                        ct_ref[sub:sub + 1, cols] = jnp.sum(jnp.where(head_row == 2 * p_idx + sub, ct, 0.0),
                                                            axis=0, keepdims=True)
                kn_ref[a * tk:(a + 1) * tk, :] = normed_keys(a)
                cb_ref[a] = key_decay(a)
                vta_ref[a], vtb_ref[a] = values_t(a)

        def logits(item, _):
            a, d = item
            cols = slice(a * tk, (a + 1) * tk)
            kn = kn_ref[d * tk:(d + 1) * tk, :]
            zs = []
            for sub, qt_ref in enumerate(qt_refs):
                decay = cb_ref[d, :, sub * LANES:(sub + 1) * LANES]
                z = (_dot(kn, qt_ref[:, cols]) - jnp.concatenate([decay] * (tk // LANES), axis=1)
                     + ct_ref[sub:sub + 1, cols])
                zs.append(jnp.where(causal, z, NEG_BIG) if d == a else z)
            return zs

        def probs(item, zs):
            return [jnp.exp2(z).astype(BF16) for z in zs]

        def values(item, ps):
            a, d = item
            cols = slice(a * tk, (a + 1) * tk)
            for acc_ref, vt_ref, p in zip(acc_refs, (vta_ref, vtb_ref), ps):
                if d == a:
                    acc_ref[:, cols] = _dot(vt_ref[d], p)
                else:
                    acc_ref[:, cols] += _dot(vt_ref[d], p)
            if d == 0:
                finalize(a)

        items = [(a, d) for a in range(nblk) for d in _descending(a + 1)]
        _pipeline(items, (prepare, logits, probs, values))

    def running_max_sweep():
        m = [[None] * nblk for _ in range(2)]

        def queries(d, _):
            prepare_queries(d)

        def logits(d, _):
            kn = normed_keys(d)
            cb = key_decay(d)
            zs = []
            for sub, qt_ref in enumerate(qt_refs):
                decay = cb[:, sub * LANES:(sub + 1) * LANES]
                z = _dot(kn, qt_ref[:, d * tk:]) - jnp.concatenate([decay] * ((s - d * tk) // LANES), axis=1)
                tri = jnp.where(causal, z[:, :tk], NEG_BIG)
                zs.append(tri if d == nblk - 1 else jnp.concatenate([tri, z[:, tk:]], axis=1))
            return zs

        def probs(d, zs):
            return _softmax_probs(d, zs, m, tk)

        def values(d, ps):
            _accumulate(ps, acc_refs, values_t(d), tk)

        _pipeline(_descending(nblk), (queries, logits, probs, values))
        for a in range(nblk):
            finalize(a)

    is_bounded = bounded_ref[0] > 0
    pl.when(is_bounded)(bounded_sweep)
    pl.when(jnp.logical_not(is_bounded))(running_max_sweep)


def _block_diag_ones():
    blk = lax.broadcasted_iota(jnp.int32, (LANES, LANES), 0) // HEAD_DIM
    return (blk == lax.broadcasted_iota(jnp.int32, (LANES, LANES), 1) // HEAD_DIM).astype(BF16)


def _fox_call(proj, c, gq, gk, n_pairs, q_col, k_col, v_col):
    b, s, _ = proj.shape
    tk = _key_tile(s)
    tok = lambda col: pl.BlockSpec((1, s, LANES), lambda i, p: (i, 0, col + p))
    const = lambda shape: pl.BlockSpec(shape, lambda i, p: (0,) * len(shape))
    rows_aug = HEAD_DIM + ONES_ROWS
    return pl.pallas_call(
        _fox_kernel, grid=(b, n_pairs),
        in_specs=[pl.BlockSpec(memory_space=pltpu.SMEM),
                  tok(q_col), tok(k_col), tok(v_col), pl.BlockSpec((1, s, LANES), lambda i, p: (i, 0, 0)),
                  const((1, LANES)), const((1, LANES)), const((LANES, LANES))],
        out_specs=tok(0),
        out_shape=jax.ShapeDtypeStruct((b, s, n_pairs * LANES), BF16),
        scratch_shapes=[pltpu.VMEM((LANES, s), BF16), pltpu.VMEM((LANES, s), BF16),
                        pltpu.VMEM((rows_aug, s), F32), pltpu.VMEM((rows_aug, s), F32), pltpu.VMEM((8, s), F32),
                        pltpu.VMEM((s, LANES), BF16),
                        pltpu.VMEM((s // tk, rows_aug, tk), BF16), pltpu.VMEM((s // tk, rows_aug, tk), BF16),
                        pltpu.VMEM((s // tk, tk, 2 * LANES), F32)],
        compiler_params=_params(2), name="fox_attn")(
            _bounded_flags(gq, gk, jnp.zeros((1,), F32)), proj, proj, proj, c, gq, gk, _block_diag_ones())


def _diff_kernel(bounded_ref, q_ref, k_ref, v_ref, bias_ref, gq_ref, gk_ref, bd_ref, lam_ref, sg_ref, o_ref,
                 q1t_ref, q2t_ref, acc1_ref, acc2_ref, kn_ref, vt_ref, *, lam_init):
    s = q_ref.shape[1]
    tk = bias_ref.shape[2]
    nblk = s // tk
    dv = v_ref.shape[2]
    causal = lax.broadcasted_iota(jnp.int32, (tk, tk), 0) <= lax.broadcasted_iota(jnp.int32, (tk, tk), 1)
    lam = (jnp.exp(jnp.sum(lam_ref[0:1, :] * lam_ref[1:2, :], axis=-1, keepdims=True))
           - jnp.exp(jnp.sum(lam_ref[2:3, :] * lam_ref[3:4, :], axis=-1, keepdims=True)) + lam_init)
    out_gain = sg_ref[...] * (1.0 - lam_init)

    def prepare_queries(a):
        qn = _group_rms(q_ref[0, a * tk:(a + 1) * tk, :].astype(F32), bd_ref[...], gq_ref[...])
        _store_split_qt(qn * (HEAD_DIM ** -0.5 * LOG2E), q1t_ref, q2t_ref, a * tk)

    def normed_keys(d):
        return _group_rms(k_ref[0, d * tk:(d + 1) * tk, :].astype(F32), bd_ref[...], gk_ref[...]).astype(BF16)

    def values_t(d):
        return _with_ones(v_ref[0, d * tk:(d + 1) * tk, :].astype(F32).T)

    def biased(z, a, d):
        if d == a:
            return jnp.where(causal, z + bias_ref[0, 0], NEG_BIG)
        return z + bias_ref[0, 1] if d == a - 1 else z

    def finalize(a):
        for c in range(a * tk // LANES, (a + 1) * tk // LANES):
            cols = slice(c * LANES, (c + 1) * LANES)
            o1 = acc1_ref[0:dv, cols] / acc1_ref[dv:dv + 1, cols]
            o2 = acc2_ref[0:dv, cols] / acc2_ref[dv:dv + 1, cols]
            o = o1 - lam * o2
            o = o * lax.rsqrt(jnp.mean(o * o, axis=0, keepdims=True) + RMS_EPS)
            o_ref[0, cols, :] = (o.T * out_gain).astype(o_ref.dtype)

    def bounded_sweep():
        def prepare(item, _):
            a, d = item
            if d == a:
                prepare_queries(a)
                kn_ref[a * tk:(a + 1) * tk, :] = normed_keys(a)
                vt_ref[a] = values_t(a)

        def logits(item, _):
            a, d = item
            kn = kn_ref[d * tk:(d + 1) * tk, :]
            return [biased(_dot(kn, qt_ref[:, a * tk:(a + 1) * tk]), a, d) for qt_ref in (q1t_ref, q2t_ref)]

        def probs(item, zs):
            return [jnp.exp2(z).astype(BF16) for z in zs]

        def values(item, ps):
            a, d = item
            cols = slice(a * tk, (a + 1) * tk)
            for acc_ref, p in zip((acc1_ref, acc2_ref), ps):
                if d == a:
                    acc_ref[:, cols] = _dot(vt_ref[d], p)
                else:
                    acc_ref[:, cols] += _dot(vt_ref[d], p)
            if d == 0:
                finalize(a)

        items = [(a, d) for a in range(nblk) for d in _descending(a + 1)]
        _pipeline(items, (prepare, logits, probs, values), lag=3)

    def running_max_sweep():
        m = [[None] * nblk for _ in range(2)]

        def queries(d, _):
            prepare_queries(d)

        def logits(d, _):
            kn = normed_keys(d)
            zs = []
            for qt_ref in (q1t_ref, q2t_ref):
                z = _dot(kn, qt_ref[:, d * tk:])
                zs.append(jnp.concatenate([biased(z[:, (a - d) * tk:(a - d + 1) * tk], a, d) for a in range(d, min(d + 2, nblk))]
                                          + ([z[:, 2 * tk:]] if d < nblk - 2 else []), axis=1))
            return zs

        def probs(d, zs):
            return _softmax_probs(d, zs, m, tk)

        def values(d, ps):
            vt = values_t(d)
            _accumulate(ps, (acc1_ref, acc2_ref), (vt, vt), tk)

        _pipeline(_descending(nblk), (queries, logits, probs, values))
        for a in range(nblk):
            finalize(a)

    is_bounded = bounded_ref[pl.program_id(1)] > 0
    pl.when(is_bounded)(bounded_sweep)
    pl.when(jnp.logical_not(is_bounded))(running_max_sweep)


def _diff_call(proj, bias, gq, gk, lam_rows, subln_g, n_heads, lam_init):
    b, s, _ = proj.shape
    tk = _key_tile(s)
    tok = lambda col: pl.BlockSpec((1, s, LANES), lambda i, p: (i, 0, col + p))
    const = lambda shape: pl.BlockSpec(shape, lambda i, p: (0,) * len(shape))
    dv = 2 * HEAD_DIM
    assert dv == LANES
    rows_aug = dv + ONES_ROWS
    return pl.pallas_call(
        functools.partial(_diff_kernel, lam_init=lam_init), grid=(b, n_heads),
        in_specs=[pl.BlockSpec(memory_space=pltpu.SMEM), tok(0), tok(n_heads), tok(2 * n_heads),
                  pl.BlockSpec((1, 2, tk, tk), lambda i, p: (p, 0, 0, 0)),
                  const((1, LANES)), const((1, LANES)), const((LANES, LANES)), const((8, LANES)), const((1, dv))],
        out_specs=tok(0),
        out_shape=jax.ShapeDtypeStruct((b, s, n_heads * dv), BF16),
        scratch_shapes=[pltpu.VMEM((LANES, s), BF16), pltpu.VMEM((LANES, s), BF16),
                        pltpu.VMEM((rows_aug, s), F32), pltpu.VMEM((rows_aug, s), F32),
                        pltpu.VMEM((s, LANES), BF16), pltpu.VMEM((s // tk, rows_aug, tk), BF16)],
        compiler_params=_params(2), name="diff_attn")(
            _bounded_flags(gq, gk, jnp.max(jnp.abs(bias), axis=(1, 2, 3))), proj, proj, proj, bias, gq, gk,
            _block_diag_ones(), lam_rows, subln_g)


def _t5_bucket(dist):
    max_exact = N_BUCKETS // 2
    nf = jnp.maximum(dist, 1).astype(F32)
    large = max_exact + (jnp.log(nf / max_exact) / math.log(MAX_DISTANCE / max_exact)
                         * (N_BUCKETS - max_exact)).astype(jnp.int32)
    large = jnp.minimum(large, N_BUCKETS - 1)
    return jnp.where(dist < max_exact, dist, large)


def _bias_tiles(rel_bias, t):
    assert t >= MAX_DISTANCE
    table = rel_bias.astype(F32)
    krow = lax.broadcasted_iota(jnp.int32, (2, t, t), 1)
    qcol = lax.broadcasted_iota(jnp.int32, (2, t, t), 2)
    dist = qcol - krow + t * lax.broadcasted_iota(jnp.int32, (2, t, t), 0)
    onehot = (_t5_bucket(jnp.maximum(dist, 0))[..., None] == jnp.arange(N_BUCKETS)).astype(F32)
    return jnp.einsum('otkb,bh->hotk', onehot, (table - table[N_BUCKETS - 1]) * LOG2E,
                      precision=lax.Precision.HIGHEST)


def _post_kernel(*refs, n_mix):
    x_ref = refs[0]
    o_refs = refs[1:1 + n_mix]
    wo_ref, g_ref, wg_ref, wu_ref, wd_ref, out_ref, x1_ref, act_ref = refs[1 + n_mix:]
    tm, d = x_ref.shape[1:]
    d_ff = wg_ref.shape[1]
    halves = _row_groups(tm)
    for rows in halves:
        o = jnp.concatenate([o_ref[0, rows, :] for o_ref in o_refs], axis=1) if n_mix > 1 else o_refs[0][0, rows, :]
        for lo in range(0, d, PROJ_COL_TILE):
            cols = slice(lo, lo + PROJ_COL_TILE)
            x1_ref[rows, cols] = x_ref[0, rows, cols] + _dot(o, wo_ref[:, cols])
    for rows in halves:
        h = _rms_rows(x1_ref[rows, :], g_ref[...]).astype(BF16)
        for lo in range(0, d_ff, FFN_COL_TILE):
            cols = slice(lo, min(lo + FFN_COL_TILE, d_ff))
            a = _dot(h, wg_ref[:, cols])
            u = _dot(h, wu_ref[:, cols])
            act_ref[rows, cols] = (a * (1.0 / (1.0 + jnp.exp(-a))) * u).astype(BF16)
    for rows in halves:
        for lo in range(0, d, PROJ_COL_TILE):
            cols = slice(lo, lo + PROJ_COL_TILE)
            out_ref[0, rows, cols] = x1_ref[rows, cols] + _dot(act_ref[rows, :], wd_ref[:, cols])


def _post_call(x, mixes, w_out, g, wg, wu, wd):
    b, s, d = x.shape
    tm = min(TOKEN_TILE, s)
    n_mix = len(mixes)
    tok = lambda width: pl.BlockSpec((1, tm, width), lambda i, j: (i, j, 0))
    res = lambda shape: pl.BlockSpec(shape, lambda i, j: (0, 0), pipeline_mode=pl.Buffered(1))
    return pl.pallas_call(
        functools.partial(_post_kernel, n_mix=n_mix), grid=(b, s // tm),
        in_specs=([tok(d)] + [tok(m.shape[2]) for m in mixes]
                  + [res(w_out.shape), res((1, d)), res(wg.shape), res(wu.shape), res(wd.shape)]),
        out_specs=tok(d),
        out_shape=jax.ShapeDtypeStruct((b, s, d), x.dtype),
        scratch_shapes=[pltpu.VMEM((tm, d), F32), pltpu.VMEM((tm, wg.shape[1]), BF16)],
        compiler_params=_params(2), name="post")(x, *mixes, w_out, g, wg, wu, wd)


def _pair_gain(g):
    return jnp.concatenate([g, g]).astype(F32)[None, :]


def kernel(x, attn_norm_g, ffn_norm_g, even_w_in, fox_forget_b, fox_q_norm_g, fox_k_norm_g, even_w_out, diff_w_in, diff_q_norm_g, diff_k_norm_g, diff_lambda_q1, diff_lambda_k1, diff_lambda_q2, diff_lambda_k2, diff_subln_g, diff_w_out, rel_bias, ffn_w_gate, ffn_w_up, ffn_w_down):
    b, s, d = x.shape
    depth = attn_norm_g.shape[0]
    n_sb = d // (2 * HEAD_DIM)
    n_fox = d // (2 * HEAD_DIM)
    n_diff = d // (2 * HEAD_DIM)
    sb_w = n_sb * HEAD_DIM
    fox_w = n_fox * HEAD_DIM
    main_w = 3 * sb_w + 3 * fox_w
    assert sb_w % LANES == 0 and fox_w % LANES == 0 and n_fox <= LANES
    bias = _bias_tiles(rel_bias, _key_tile(s))

    for layer in range(depth):
        g_attn = attn_norm_g[layer][None, :]
        if layer % 2 == 0:
            e = layer // 2
            w_in = even_w_in[e]
            wfg = jnp.pad(w_in[:, main_w:], ((0, 0), (0, LANES - n_fox))).astype(BF16)
            fb = jnp.pad(fox_forget_b[e].astype(F32), (0, LANES - n_fox))[None, :]
            proj, c = _pre_call(x, g_attn, w_in[:, :main_w].astype(BF16), (wfg, fb))
            sbc, fxc = sb_w // LANES, fox_w // LANES
            o_a = _sb_call(proj, sbc, 0, sbc, 2 * sbc)
            o_b = _fox_call(proj, c, _pair_gain(fox_q_norm_g[e]), _pair_gain(fox_k_norm_g[e]),
                            fxc, 3 * sbc, 3 * sbc + fxc, 3 * sbc + 2 * fxc)
            mixes, w_out = [o_a, o_b], even_w_out[e].astype(BF16)
        else:
            o = layer // 2
            proj = _pre_call(x, g_attn, diff_w_in[o].astype(BF16))
            lam_rows = jnp.stack([diff_lambda_q1[o], diff_lambda_k1[o], diff_lambda_q2[o], diff_lambda_k2[o]])
            lam_rows = jnp.pad(lam_rows.astype(F32), ((0, 4), (0, LANES - HEAD_DIM)))
            lam_init = 0.8 - 0.6 * math.exp(-0.3 * layer)
            mix = _diff_call(proj, bias, _pair_gain(diff_q_norm_g[o]), _pair_gain(diff_k_norm_g[o]),
                             lam_rows, diff_subln_g[o].astype(F32)[None, :], n_diff, lam_init)
            mixes, w_out = [mix], diff_w_out[o].astype(BF16)
        x = _post_call(x, mixes, w_out, ffn_norm_g[layer][None, :], ffn_w_gate[layer].astype(BF16),
                       ffn_w_up[layer].astype(BF16), ffn_w_down[layer].astype(BF16))
    return x
```

```python
import functools
import math

import jax
import jax.numpy as jnp
import numpy as np
from jax import lax
from jax.experimental import pallas as pl
from jax.experimental.pallas import tpu as pltpu

F32 = jnp.float32
BF16 = jnp.bfloat16

HEAD_DIM = 64
N_BUCKETS = 32
MAX_DISTANCE = 128
RMS_EPS = 1e-6
LOG2E = 1.4426950408889634
NEG_BIG = -1e30
SIGN_BIT = np.uint32(0x80000000)

LANES = 128
K_TILE = 256
TOKEN_TILE = 1024
PRE_TOKEN_TILE = 1024
PROJ_COL_TILE = 512
FFN_COL_TILE = 512
ROW_GROUP = 256
ONES_ROWS = 16
SB_K_TILE = 128
SB_NEAR_TILES = 3
SB_ZERO_BITS = 152.0
SAFE_LOGIT_BITS = 30.0
QK_BOUND_SLACK = 1.02
VMEM_LIMIT = 56 * 1024 * 1024


def _dot(a, b):
    return jnp.dot(a, b, preferred_element_type=F32)


def _params(n_axes, vmem=VMEM_LIMIT):
    return pltpu.CompilerParams(dimension_semantics=("arbitrary",) * n_axes, vmem_limit_bytes=vmem)


def _rms_rows(x, g):
    ms = jnp.mean(x * x, axis=-1, keepdims=True)
    return x * lax.rsqrt(ms + RMS_EPS) * g


def _split3(x):
    hi = x.astype(BF16)
    r = x - hi.astype(F32)
    mid = r.astype(BF16)
    lo = (r - mid.astype(F32)).astype(BF16)
    return hi, mid, lo


def _row_groups(tm):
    step = min(ROW_GROUP, tm)
    return [slice(r, r + step) for r in range(0, tm, step)]


def _project(h, rows, w_ref, o_ref, col_tiles):
    for j in col_tiles:
        cols = slice(j * PROJ_COL_TILE, (j + 1) * PROJ_COL_TILE)
        o_ref[0, rows, cols] = _dot(h, w_ref[:, cols]).astype(o_ref.dtype)


def _pre_odd_kernel(x_ref, g_ref, w_ref, o_ref):
    groups = _row_groups(x_ref.shape[1])
    hs = [_rms_rows(x_ref[0, rows, :], g_ref[...]).astype(BF16) for rows in groups]
    for rows, h in zip(groups, hs):
        _project(h, rows, w_ref, o_ref, range(w_ref.shape[1] // PROJ_COL_TILE))


def _pre_even_kernel(x_ref, g_ref, w_ref, wfg_ref, fb_ref, ltri_ref, o_ref, c_ref, carry_ref):
    @pl.when(pl.program_id(1) == 0)
    def _():
        carry_ref[...] = jnp.zeros_like(carry_ref)

    groups = _row_groups(x_ref.shape[1])
    n_tiles = w_ref.shape[1] // PROJ_COL_TILE
    tc = ltri_ref.shape[0]
    assert groups[0].stop % tc == 0
    hs = [_rms_rows(x_ref[0, rows, :], g_ref[...]).astype(BF16) for rows in groups]
    pieces = []
    for rows, h in zip(groups, hs):
        fg = _dot(h, wfg_ref[...]) + fb_ref[...]
        logf = jnp.minimum(fg, 0.0) - jnp.log1p(jnp.exp(-jnp.abs(fg)))
        pieces.append([_split3(logf[r:r + tc]) for r in range(0, logf.shape[0], tc)])
    for rows, h in zip(groups, hs):
        _project(h, rows, w_ref, o_ref, range(n_tiles // 2))
    carry = carry_ref[...]
    for rows, chunks in zip(groups, pieces):
        for ci, (hi, mid, lo) in enumerate(chunks):
            cs = _dot(ltri_ref[...], hi) + _dot(ltri_ref[...], mid) + _dot(ltri_ref[...], lo) + carry
            c_ref[0, rows.start + ci * tc:rows.start + (ci + 1) * tc, :] = cs
            carry = cs[tc - 1:tc, :]
    carry_ref[...] = carry
    for rows, h in zip(groups, hs):
        _project(h, rows, w_ref, o_ref, range(n_tiles // 2, n_tiles))


def _pre_call(x, g, w, extra=None):
    b, s, d = x.shape
    n = w.shape[1]
    tm = min(PRE_TOKEN_TILE, s)
    grid = (b, s // tm)
    x_spec = pl.BlockSpec((1, tm, d), lambda i, j: (i, j, 0))
    g_spec = pl.BlockSpec((1, d), lambda i, j: (0, 0))
    w_spec = pl.BlockSpec((d, n), lambda i, j: (0, 0))
    o_spec = pl.BlockSpec((1, tm, n), lambda i, j: (i, j, 0))
    o_shape = jax.ShapeDtypeStruct((b, s, n), BF16)
    if extra is None:
        return pl.pallas_call(
            _pre_odd_kernel, grid=grid, in_specs=[x_spec, g_spec, w_spec], out_specs=o_spec,
            out_shape=o_shape, compiler_params=_params(2), name="pre_odd")(x, g, w)
    wfg, fb = extra
    tc = min(K_TILE, tm)
    ltri = (lax.broadcasted_iota(jnp.int32, (tc, tc), 0) >= lax.broadcasted_iota(jnp.int32, (tc, tc), 1)).astype(BF16)
    return pl.pallas_call(
        _pre_even_kernel, grid=grid,
        in_specs=[x_spec, g_spec, w_spec,
                  pl.BlockSpec((d, LANES), lambda i, j: (0, 0)),
                  pl.BlockSpec((1, LANES), lambda i, j: (0, 0)),
                  pl.BlockSpec((tc, tc), lambda i, j: (0, 0))],
        out_specs=[o_spec, pl.BlockSpec((1, tm, LANES), lambda i, j: (i, j, 0))],
        out_shape=[o_shape, jax.ShapeDtypeStruct((b, s, LANES), F32)],
        scratch_shapes=[pltpu.VMEM((1, LANES), F32)],
        compiler_params=_params(2), name="pre_even")(x, g, w, wfg, fb, ltri)


def _key_tile(s):
    tk = min(K_TILE, s)
    assert s % tk == 0 and tk % LANES == 0
    return tk


def _pipeline(items, stages, lag=1):
    carried = [{} for _ in stages]
    for step in range(len(items) + lag * (len(stages) - 1)):
        for i, stage in enumerate(stages):
            idx = step - i * lag
            if 0 <= idx < len(items):
                arg = carried[i - 1].pop(idx) if i else None
                carried[i][idx] = stage(items[idx], arg)


def _descending(n):
    return list(reversed(range(n)))


def _store_split_qt(qn, qat_ref, qbt_ref, col0):
    row = lax.broadcasted_iota(jnp.int32, (LANES, LANES), 0)
    for c in range(qn.shape[0] // LANES):
        cols = slice(col0 + c * LANES, col0 + (c + 1) * LANES)
        qt = qn[c * LANES:(c + 1) * LANES].T
        qat_ref[:, cols] = jnp.where(row < HEAD_DIM, qt, 0.0).astype(BF16)
        qbt_ref[:, cols] = jnp.where(row >= HEAD_DIM, qt, 0.0).astype(BF16)


def _group_rms(x, bd, g):
    ss = _dot((x * x).astype(BF16), bd)
    return x * lax.rsqrt(ss * (1.0 / HEAD_DIM) + RMS_EPS) * g


def _softmax_probs(d, zs, m, tk):
    out = []
    for cb in range(zs[0].shape[1] // tk):
        a = d + cb
        for i, z in enumerate(zs):
            zb = z[:, cb * tk:(cb + 1) * tk]
            m_cur = jnp.max(zb, axis=0, keepdims=True)
            if a == d:
                out.append((a, i, jnp.exp2(zb - m_cur).astype(BF16), None))
                m[i][a] = m_cur
            else:
                m_new = jnp.maximum(m[i][a], m_cur)
                out.append((a, i, jnp.exp2(zb - m_new).astype(BF16), jnp.exp2(m[i][a] - m_new)))
                m[i][a] = m_new
    return out


def _accumulate(probs, acc_refs, vts, tk):
    for a, i, p, alpha in probs:
        cols = slice(a * tk, (a + 1) * tk)
        if alpha is None:
            acc_refs[i][:, cols] = _dot(vts[i], p)
        else:
            acc_refs[i][:, cols] = acc_refs[i][:, cols] * alpha + _dot(vts[i], p)


def _with_ones(vt):
    return jnp.concatenate([vt.astype(BF16), jnp.ones((ONES_ROWS, vt.shape[1]), BF16)], axis=0)


def _sb_kernel(q_ref, k_ref, v_ref, u_ref, o_ref, qat_ref, qbt_ref, acc_ref, r_ref):
    s = q_ref.shape[1]
    tk = u_ref.shape[0]
    nblk = s // tk
    strict = lax.broadcasted_iota(jnp.int32, (tk, tk), 0) < lax.broadcasted_iota(jnp.int32, (tk, tk), 1)
    r = [[None] * nblk for _ in range(2)]

    def suffix_sum(z, tri):
        neg_abs = lax.bitcast_convert_type(lax.bitcast_convert_type(z, jnp.uint32) | SIGN_BIT, F32)
        sp = jnp.maximum(z, 0.0) + LOG2E * jnp.log(1.0 + jnp.exp2(neg_abs))
        if tri:
            sp = jnp.where(strict, sp, 0.0)
        cum = _dot(u_ref[...], sp.astype(BF16))
        return z - sp, cum, cum[0:1, :] + sp[0:1, :]

    def queries(d, _):
        qs = q_ref[0, d * tk:(d + 1) * tk, :].astype(F32) * (HEAD_DIM ** -0.5 * LOG2E)
        _store_split_qt(qs, qat_ref, qbt_ref, d * tk)

    def logits(d, _):
        kblk = k_ref[0, d * tk:(d + 1) * tk, :]
        cols = slice(d * tk, min(d + SB_NEAR_TILES, nblk) * tk)
        return [_dot(kblk, qt_ref[:, cols]) for qt_ref in (qat_ref, qbt_ref)]

    def suffix_sums(d, zs):
        chains = []
        for cb in range(zs[0].shape[1] // tk):
            for sub in range(2):
                chains.append((cb, sub) + suffix_sum(zs[sub][:, cb * tk:(cb + 1) * tk], cb == 0))
        return chains

    def values(d, chains):
        vt = v_ref[0, d * tk:(d + 1) * tk, :].astype(F32).T.astype(BF16)
        for cb, sub, log_beta, cum, total in chains:
            a = d + cb
            rows = slice(sub * HEAD_DIM, (sub + 1) * HEAD_DIM)
            cols = slice(a * tk, (a + 1) * tk)
            if cb == 0:
                w = jnp.where(strict, jnp.exp2(log_beta - cum), 0.0)
                acc_ref[rows, cols] = _dot(vt[rows], w.astype(BF16))
                r[sub][a] = total
            else:
                w = jnp.exp2(log_beta - cum - r[sub][a])
                acc_ref[rows, cols] += _dot(vt[rows], w.astype(BF16))
                r[sub][a] = r[sub][a] + total

    _pipeline(_descending(nblk), (queries, logits, suffix_sums, values))

    def far_tile(d):
        kblk = k_ref[0, d * tk:(d + 1) * tk, :]
        vt = v_ref[0, d * tk:(d + 1) * tk, :].astype(F32).T.astype(BF16)
        for a in range(d + SB_NEAR_TILES, nblk):
            cols = slice(a * tk, (a + 1) * tk)
            for sub, qt_ref in enumerate((qat_ref, qbt_ref)):
                rows = slice(sub * HEAD_DIM, (sub + 1) * HEAD_DIM)
                log_beta, cum, total = suffix_sum(_dot(kblk, qt_ref[:, cols]), False)
                r_old = r_ref[sub:sub + 1, cols]
                acc_ref[rows, cols] += _dot(vt[rows], jnp.exp2(log_beta - cum - r_old).astype(BF16))
                r_ref[sub:sub + 1, cols] = r_old + total

    if nblk > SB_NEAR_TILES:
        for sub in range(2):
            r_ref[sub:sub + 1, :] = jnp.concatenate(r[sub], axis=1)
        @pl.when(jnp.min(r_ref[0:2, SB_NEAR_TILES * tk:]) < SB_ZERO_BITS)
        def _():
            for d in reversed(range(nblk - SB_NEAR_TILES)):
                @pl.when(jnp.min(r_ref[0:2, (d + SB_NEAR_TILES) * tk:]) < SB_ZERO_BITS)
                def _(d=d):
                    far_tile(d)

    for c in range(s // LANES):
        cols = slice(c * LANES, (c + 1) * LANES)
        o_ref[0, cols, :] = acc_ref[:, cols].T.astype(o_ref.dtype)


def _sb_call(proj, n_pairs, q_col, k_col, v_col):
    b, s, _ = proj.shape
    tk = min(SB_K_TILE, s)
    assert s % tk == 0 and tk % LANES == 0
    u = (lax.broadcasted_iota(jnp.int32, (tk, tk), 1) > lax.broadcasted_iota(jnp.int32, (tk, tk), 0)).astype(BF16)
    tok = lambda col: pl.BlockSpec((1, s, LANES), lambda i, p: (i, 0, col + p))
    return pl.pallas_call(
        _sb_kernel, grid=(b, n_pairs),
        in_specs=[tok(q_col), tok(k_col), tok(v_col), pl.BlockSpec((tk, tk), lambda i, p: (0, 0))],
        out_specs=tok(0),
        out_shape=jax.ShapeDtypeStruct((b, s, n_pairs * LANES), BF16),
        scratch_shapes=[pltpu.VMEM((LANES, s), BF16), pltpu.VMEM((LANES, s), BF16), pltpu.VMEM((LANES, s), F32),
                        pltpu.VMEM((8, s), F32)],
        compiler_params=_params(2), name="sb_attn")(proj, proj, proj, u)


def _bounded_flags(gq, gk, extra_bits):
    qk_bits = jnp.max(jnp.abs(gq)) * jnp.max(jnp.abs(gk)) * (HEAD_DIM ** 0.5 * LOG2E * QK_BOUND_SLACK)
    return (qk_bits + extra_bits <= SAFE_LOGIT_BITS).astype(jnp.int32)


def _fox_kernel(bounded_ref, q_ref, k_ref, v_ref, c_ref, gq_ref, gk_ref, bd_ref, o_ref,
                qat_ref, qbt_ref, acca_ref, accb_ref, ct_ref, kn_ref, vta_ref, vtb_ref, cb_ref):
    s = q_ref.shape[1]
    tk = _key_tile(s)
    nblk = s // tk
    p_idx = pl.program_id(1)
    sel_r = lax.broadcasted_iota(jnp.int32, (LANES, 2 * LANES), 0)
    sel_l = lax.broadcasted_iota(jnp.int32, (LANES, 2 * LANES), 1)
    sel = (sel_r == 2 * p_idx + (sel_l >= LANES).astype(jnp.int32)).astype(BF16)
    causal = lax.broadcasted_iota(jnp.int32, (tk, tk), 0) <= lax.broadcasted_iota(jnp.int32, (tk, tk), 1)
    head_row = lax.broadcasted_iota(jnp.int32, (LANES, LANES), 0)
    qt_refs, acc_refs = (qat_ref, qbt_ref), (acca_ref, accb_ref)

    def prepare_queries(a):
        qn = _group_rms(q_ref[0, a * tk:(a + 1) * tk, :].astype(F32), bd_ref[...], gq_ref[...])
        _store_split_qt(qn * (HEAD_DIM ** -0.5 * LOG2E), qat_ref, qbt_ref, a * tk)

    def normed_keys(d):
        return _group_rms(k_ref[0, d * tk:(d + 1) * tk, :].astype(F32), bd_ref[...], gk_ref[...]).astype(BF16)

    def key_decay(d):
        hi, mid, lo = _split3(c_ref[0, d * tk:(d + 1) * tk, :] * LOG2E)
        return _dot(hi, sel) + _dot(mid, sel) + _dot(lo, sel)

    def values_t(d):
        vt = v_ref[0, d * tk:(d + 1) * tk, :].astype(F32).T
        return _with_ones(vt[:HEAD_DIM]), _with_ones(vt[HEAD_DIM:])

    def finalize(a):
        for c in range(a * tk // LANES, (a + 1) * tk // LANES):
            cols = slice(c * LANES, (c + 1) * LANES)
            oa = acca_ref[0:HEAD_DIM, cols] / acca_ref[HEAD_DIM:HEAD_DIM + 1, cols]
            ob = accb_ref[0:HEAD_DIM, cols] / accb_ref[HEAD_DIM:HEAD_DIM + 1, cols]
            o_ref[0, cols, :] = jnp.concatenate([oa, ob], axis=0).T.astype(o_ref.dtype)

    def bounded_sweep():
        def prepare(item, _):
            a, d = item
            if d == a:
                prepare_queries(a)
                for c in range(a * tk // LANES, (a + 1) * tk // LANES):
                    cols = slice(c * LANES, (c + 1) * LANES)
                    ct = (c_ref[0, cols, :] * LOG2E).T
                    for sub in range(2):
                        ct_ref[sub:sub + 1, cols] = jnp.sum(jnp.where(head_row == 2 * p_idx + sub, ct, 0.0),
                                                            axis=0, keepdims=True)
                kn_ref[a * tk:(a + 1) * tk, :] = normed_keys(a)
                cb_ref[a] = key_decay(a)
                vta_ref[a], vtb_ref[a] = values_t(a)

        def logits(item, _):
            a, d = item
            cols = slice(a * tk, (a + 1) * tk)
            kn = kn_ref[d * tk:(d + 1) * tk, :]
            zs = []
            for sub, qt_ref in enumerate(qt_refs):
                decay = cb_ref[d, :, sub * LANES:(sub + 1) * LANES]
                z = (_dot(kn, qt_ref[:, cols]) - jnp.concatenate([decay] * (tk // LANES), axis=1)
                     + ct_ref[sub:sub + 1, cols])
                zs.append(jnp.where(causal, z, NEG_BIG) if d == a else z)
            return zs

        def probs(item, zs):
            return [jnp.exp2(z).astype(BF16) for z in zs]

        def values(item, ps):
            a, d = item
            cols = slice(a * tk, (a + 1) * tk)
            for acc_ref, vt_ref, p in zip(acc_refs, (vta_ref, vtb_ref), ps):
                if d == a:
                    acc_ref[:, cols] = _dot(vt_ref[d], p)
                else:
                    acc_ref[:, cols] += _dot(vt_ref[d], p)
            if d == 0:
                finalize(a)

        items = [(a, d) for a in range(nblk) for d in _descending(a + 1)]
        _pipeline(items, (prepare, logits, probs, values))

    def running_max_sweep():
        m = [[None] * nblk for _ in range(2)]

        def queries(d, _):
            prepare_queries(d)

        def logits(d, _):
            kn = normed_keys(d)
            cb = key_decay(d)
            zs = []
            for sub, qt_ref in enumerate(qt_refs):
                decay = cb[:, sub * LANES:(sub + 1) * LANES]
                z = _dot(kn, qt_ref[:, d * tk:]) - jnp.concatenate([decay] * ((s - d * tk) // LANES), axis=1)
                tri = jnp.where(causal, z[:, :tk], NEG_BIG)
                zs.append(tri if d == nblk - 1 else jnp.concatenate([tri, z[:, tk:]], axis=1))
            return zs

        def probs(d, zs):
            return _softmax_probs(d, zs, m, tk)

        def values(d, ps):
            _accumulate(ps, acc_refs, values_t(d), tk)

        _pipeline(_descending(nblk), (queries, logits, probs, values))
        for a in range(nblk):
            finalize(a)

    is_bounded = bounded_ref[0] > 0
    pl.when(is_bounded)(bounded_sweep)
    pl.when(jnp.logical_not(is_bounded))(running_max_sweep)


def _block_diag_ones():
    blk = lax.broadcasted_iota(jnp.int32, (LANES, LANES), 0) // HEAD_DIM
    return (blk == lax.broadcasted_iota(jnp.int32, (LANES, LANES), 1) // HEAD_DIM).astype(BF16)


def _fox_call(proj, c, gq, gk, n_pairs, q_col, k_col, v_col):
    b, s, _ = proj.shape
    tk = _key_tile(s)
    tok = lambda col: pl.BlockSpec((1, s, LANES), lambda i, p: (i, 0, col + p))
    const = lambda shape: pl.BlockSpec(shape, lambda i, p: (0,) * len(shape))
    rows_aug = HEAD_DIM + ONES_ROWS
    return pl.pallas_call(
        _fox_kernel, grid=(b, n_pairs),
        in_specs=[pl.BlockSpec(memory_space=pltpu.SMEM),
                  tok(q_col), tok(k_col), tok(v_col), pl.BlockSpec((1, s, LANES), lambda i, p: (i, 0, 0)),
                  const((1, LANES)), const((1, LANES)), const((LANES, LANES))],
        out_specs=tok(0),
        out_shape=jax.ShapeDtypeStruct((b, s, n_pairs * LANES), BF16),
        scratch_shapes=[pltpu.VMEM((LANES, s), BF16), pltpu.VMEM((LANES, s), BF16),
                        pltpu.VMEM((rows_aug, s), F32), pltpu.VMEM((rows_aug, s), F32), pltpu.VMEM((8, s), F32),
                        pltpu.VMEM((s, LANES), BF16),
                        pltpu.VMEM((s // tk, rows_aug, tk), BF16), pltpu.VMEM((s // tk, rows_aug, tk), BF16),
                        pltpu.VMEM((s // tk, tk, 2 * LANES), F32)],
        compiler_params=_params(2), name="fox_attn")(
            _bounded_flags(gq, gk, jnp.zeros((1,), F32)), proj, proj, proj, c, gq, gk, _block_diag_ones())


def _diff_kernel(bounded_ref, q_ref, k_ref, v_ref, bias_ref, gq_ref, gk_ref, bd_ref, lam_ref, sg_ref, o_ref,
                 q1t_ref, q2t_ref, acc1_ref, acc2_ref, kn_ref, vt_ref, *, lam_init):
    s = q_ref.shape[1]
    tk = bias_ref.shape[2]
    nblk = s // tk
    dv = v_ref.shape[2]
    causal = lax.broadcasted_iota(jnp.int32, (tk, tk), 0) <= lax.broadcasted_iota(jnp.int32, (tk, tk), 1)
    lam = (jnp.exp(jnp.sum(lam_ref[0:1, :] * lam_ref[1:2, :], axis=-1, keepdims=True))
           - jnp.exp(jnp.sum(lam_ref[2:3, :] * lam_ref[3:4, :], axis=-1, keepdims=True)) + lam_init)
    out_gain = sg_ref[...] * (1.0 - lam_init)

    def prepare_queries(a):
        qn = _group_rms(q_ref[0, a * tk:(a + 1) * tk, :].astype(F32), bd_ref[...], gq_ref[...])
        _store_split_qt(qn * (HEAD_DIM ** -0.5 * LOG2E), q1t_ref, q2t_ref, a * tk)

    def normed_keys(d):
        return _group_rms(k_ref[0, d * tk:(d + 1) * tk, :].astype(F32), bd_ref[...], gk_ref[...]).astype(BF16)

    def values_t(d):
        return _with_ones(v_ref[0, d * tk:(d + 1) * tk, :].astype(F32).T)

    def biased(z, a, d):
        if d == a:
            return jnp.where(causal, z + bias_ref[0, 0], NEG_BIG)
        return z + bias_ref[0, 1] if d == a - 1 else z

    def finalize(a):
        for c in range(a * tk // LANES, (a + 1) * tk // LANES):
            cols = slice(c * LANES, (c + 1) * LANES)
            o1 = acc1_ref[0:dv, cols] / acc1_ref[dv:dv + 1, cols]
            o2 = acc2_ref[0:dv, cols] / acc2_ref[dv:dv + 1, cols]
            o = o1 - lam * o2
            o = o * lax.rsqrt(jnp.mean(o * o, axis=0, keepdims=True) + RMS_EPS)
            o_ref[0, cols, :] = (o.T * out_gain).astype(o_ref.dtype)

    def bounded_sweep():
        def prepare(item, _):
            a, d = item
            if d == a:
                prepare_queries(a)
                kn_ref[a * tk:(a + 1) * tk, :] = normed_keys(a)
                vt_ref[a] = values_t(a)

        def logits(item, _):
            a, d = item
            kn = kn_ref[d * tk:(d + 1) * tk, :]
            return [biased(_dot(kn, qt_ref[:, a * tk:(a + 1) * tk]), a, d) for qt_ref in (q1t_ref, q2t_ref)]

        def probs(item, zs):
            return [jnp.exp2(z).astype(BF16) for z in zs]

        def values(item, ps):
            a, d = item
            cols = slice(a * tk, (a + 1) * tk)
            for acc_ref, p in zip((acc1_ref, acc2_ref), ps):
                if d == a:
                    acc_ref[:, cols] = _dot(vt_ref[d], p)
                else:
                    acc_ref[:, cols] += _dot(vt_ref[d], p)
            if d == 0:
                finalize(a)

        items = [(a, d) for a in range(nblk) for d in _descending(a + 1)]
        _pipeline(items, (prepare, logits, probs, values), lag=3)

    def running_max_sweep():
        m = [[None] * nblk for _ in range(2)]

        def queries(d, _):
            prepare_queries(d)

        def logits(d, _):
            kn = normed_keys(d)
            zs = []
            for qt_ref in (q1t_ref, q2t_ref):
                z = _dot(kn, qt_ref[:, d * tk:])
                zs.append(jnp.concatenate([biased(z[:, (a - d) * tk:(a - d + 1) * tk], a, d) for a in range(d, min(d + 2, nblk))]
                                          + ([z[:, 2 * tk:]] if d < nblk - 2 else []), axis=1))
            return zs

        def probs(d, zs):
            return _softmax_probs(d, zs, m, tk)

        def values(d, ps):
            vt = values_t(d)
            _accumulate(ps, (acc1_ref, acc2_ref), (vt, vt), tk)

        _pipeline(_descending(nblk), (queries, logits, probs, values))
        for a in range(nblk):
            finalize(a)

    is_bounded = bounded_ref[pl.program_id(1)] > 0
    pl.when(is_bounded)(bounded_sweep)
    pl.when(jnp.logical_not(is_bounded))(running_max_sweep)


def _diff_call(proj, bias, gq, gk, lam_rows, subln_g, n_heads, lam_init):
    b, s, _ = proj.shape
    tk = _key_tile(s)
    tok = lambda col: pl.BlockSpec((1, s, LANES), lambda i, p: (i, 0, col + p))
    const = lambda shape: pl.BlockSpec(shape, lambda i, p: (0,) * len(shape))
    dv = 2 * HEAD_DIM
    assert dv == LANES
    rows_aug = dv + ONES_ROWS
    return pl.pallas_call(
        functools.partial(_diff_kernel, lam_init=lam_init), grid=(b, n_heads),
        in_specs=[pl.BlockSpec(memory_space=pltpu.SMEM), tok(0), tok(n_heads), tok(2 * n_heads),
                  pl.BlockSpec((1, 2, tk, tk), lambda i, p: (p, 0, 0, 0)),
                  const((1, LANES)), const((1, LANES)), const((LANES, LANES)), const((8, LANES)), const((1, dv))],
        out_specs=tok(0),
        out_shape=jax.ShapeDtypeStruct((b, s, n_heads * dv), BF16),
        scratch_shapes=[pltpu.VMEM((LANES, s), BF16), pltpu.VMEM((LANES, s), BF16),
                        pltpu.VMEM((rows_aug, s), F32), pltpu.VMEM((rows_aug, s), F32),
                        pltpu.VMEM((s, LANES), BF16), pltpu.VMEM((s // tk, rows_aug, tk), BF16)],
        compiler_params=_params(2), name="diff_attn")(
            _bounded_flags(gq, gk, jnp.max(jnp.abs(bias), axis=(1, 2, 3))), proj, proj, proj, bias, gq, gk,
            _block_diag_ones(), lam_rows, subln_g)


def _t5_bucket(dist):
    max_exact = N_BUCKETS // 2
    nf = jnp.maximum(dist, 1).astype(F32)
    large = max_exact + (jnp.log(nf / max_exact) / math.log(MAX_DISTANCE / max_exact)
                         * (N_BUCKETS - max_exact)).astype(jnp.int32)
    large = jnp.minimum(large, N_BUCKETS - 1)
    return jnp.where(dist < max_exact, dist, large)


def _bias_tiles(rel_bias, t):
    assert t >= MAX_DISTANCE
    table = rel_bias.astype(F32)
    krow = lax.broadcasted_iota(jnp.int32, (2, t, t), 1)
    qcol = lax.broadcasted_iota(jnp.int32, (2, t, t), 2)
    dist = qcol - krow + t * lax.broadcasted_iota(jnp.int32, (2, t, t), 0)
    onehot = (_t5_bucket(jnp.maximum(dist, 0))[..., None] == jnp.arange(N_BUCKETS)).astype(F32)
    return jnp.einsum('otkb,bh->hotk', onehot, (table - table[N_BUCKETS - 1]) * LOG2E,
                      precision=lax.Precision.HIGHEST)


def _post_kernel(*refs, n_mix):
    x_ref = refs[0]
    o_refs = refs[1:1 + n_mix]
    wo_ref, g_ref, wg_ref, wu_ref, wd_ref, out_ref, x1_ref, act_ref = refs[1 + n_mix:]
    tm, d = x_ref.shape[1:]
    d_ff = wg_ref.shape[1]
    halves = _row_groups(tm)
    for rows in halves:
        o = jnp.concatenate([o_ref[0, rows, :] for o_ref in o_refs], axis=1) if n_mix > 1 else o_refs[0][0, rows, :]
        for lo in range(0, d, PROJ_COL_TILE):
            cols = slice(lo, lo + PROJ_COL_TILE)
            x1_ref[rows, cols] = x_ref[0, rows, cols] + _dot(o, wo_ref[:, cols])
    for rows in halves:
        h = _rms_rows(x1_ref[rows, :], g_ref[...]).astype(BF16)
        for lo in range(0, d_ff, FFN_COL_TILE):
            cols = slice(lo, min(lo + FFN_COL_TILE, d_ff))
            a = _dot(h, wg_ref[:, cols])
            u = _dot(h, wu_ref[:, cols])
            act_ref[rows, cols] = (a * (1.0 / (1.0 + jnp.exp(-a))) * u).astype(BF16)
    for rows in halves:
        for lo in range(0, d, PROJ_COL_TILE):
            cols = slice(lo, lo + PROJ_COL_TILE)
            out_ref[0, rows, cols] = x1_ref[rows, cols] + _dot(act_ref[rows, :], wd_ref[:, cols])


def _post_call(x, mixes, w_out, g, wg, wu, wd):
    b, s, d = x.shape
    tm = min(TOKEN_TILE, s)
    n_mix = len(mixes)
    tok = lambda width: pl.BlockSpec((1, tm, width), lambda i, j: (i, j, 0))
    res = lambda shape: pl.BlockSpec(shape, lambda i, j: (0, 0), pipeline_mode=pl.Buffered(1))
    return pl.pallas_call(
        functools.partial(_post_kernel, n_mix=n_mix), grid=(b, s // tm),
        in_specs=([tok(d)] + [tok(m.shape[2]) for m in mixes]
                  + [res(w_out.shape), res((1, d)), res(wg.shape), res(wu.shape), res(wd.shape)]),
        out_specs=tok(d),
        out_shape=jax.ShapeDtypeStruct((b, s, d), x.dtype),
        scratch_shapes=[pltpu.VMEM((tm, d), F32), pltpu.VMEM((tm, wg.shape[1]), BF16)],
        compiler_params=_params(2), name="post")(x, *mixes, w_out, g, wg, wu, wd)


def _pair_gain(g):
    return jnp.concatenate([g, g]).astype(F32)[None, :]


def kernel(x, attn_norm_g, ffn_norm_g, even_w_in, fox_forget_b, fox_q_norm_g, fox_k_norm_g, even_w_out, diff_w_in, diff_q_norm_g, diff_k_norm_g, diff_lambda_q1, diff_lambda_k1, diff_lambda_q2, diff_lambda_k2, diff_subln_g, diff_w_out, rel_bias, ffn_w_gate, ffn_w_up, ffn_w_down):
    b, s, d = x.shape
    depth = attn_norm_g.shape[0]
    n_sb = d // (2 * HEAD_DIM)
    n_fox = d // (2 * HEAD_DIM)
    n_diff = d // (2 * HEAD_DIM)
    sb_w = n_sb * HEAD_DIM
    fox_w = n_fox * HEAD_DIM
    main_w = 3 * sb_w + 3 * fox_w
    assert sb_w % LANES == 0 and fox_w % LANES == 0 and n_fox <= LANES
    bias = _bias_tiles(rel_bias, _key_tile(s))

    for layer in range(depth):
        g_attn = attn_norm_g[layer][None, :]
        if layer % 2 == 0:
            e = layer // 2
            w_in = even_w_in[e]
            wfg = jnp.pad(w_in[:, main_w:], ((0, 0), (0, LANES - n_fox))).astype(BF16)
            fb = jnp.pad(fox_forget_b[e].astype(F32), (0, LANES - n_fox))[None, :]
            proj, c = _pre_call(x, g_attn, w_in[:, :main_w].astype(BF16), (wfg, fb))
            sbc, fxc = sb_w // LANES, fox_w // LANES
            o_a = _sb_call(proj, sbc, 0, sbc, 2 * sbc)
            o_b = _fox_call(proj, c, _pair_gain(fox_q_norm_g[e]), _pair_gain(fox_k_norm_g[e]),
                            fxc, 3 * sbc, 3 * sbc + fxc, 3 * sbc + 2 * fxc)
            mixes, w_out = [o_a, o_b], even_w_out[e].astype(BF16)
        else:
            o = layer // 2
            proj = _pre_call(x, g_attn, diff_w_in[o].astype(BF16))
            lam_rows = jnp.stack([diff_lambda_q1[o], diff_lambda_k1[o], diff_lambda_q2[o], diff_lambda_k2[o]])
            lam_rows = jnp.pad(lam_rows.astype(F32), ((0, 4), (0, LANES - HEAD_DIM)))
            lam_init = 0.8 - 0.6 * math.exp(-0.3 * layer)
            mix = _diff_call(proj, bias, _pair_gain(diff_q_norm_g[o]), _pair_gain(diff_k_norm_g[o]),
                             lam_rows, diff_subln_g[o].astype(F32)[None, :], n_diff, lam_init)
            mixes, w_out = [mix], diff_w_out[o].astype(BF16)
        x = _post_call(x, mixes, w_out, ffn_norm_g[layer][None, :], ffn_w_gate[layer].astype(BF16),
                       ffn_w_up[layer].astype(BF16), ffn_w_down[layer].astype(BF16))
    return x
```

```python
import functools
import math

import jax
import jax.numpy as jnp
import numpy as np
from jax import lax
from jax.experimental import pallas as pl
from jax.experimental.pallas import tpu as pltpu

F32 = jnp.float32
BF16 = jnp.bfloat16

HEAD_DIM = 64
N_BUCKETS = 32
MAX_DISTANCE = 128
RMS_EPS = 1e-6
LOG2E = 1.4426950408889634
NEG_BIG = -1e30
SIGN_BIT = np.uint32(0x80000000)

LANES = 128
K_TILE = 256
TOKEN_TILE = 1024
PRE_TOKEN_TILE = 1024
PROJ_COL_TILE = 512
FFN_COL_TILE = 512
ROW_GROUP = 256
ONES_ROWS = 16
SB_K_TILE = 128
SB_NEAR_TILES = 3
SB_ZERO_BITS = 152.0
SAFE_LOGIT_BITS = 30.0
QK_BOUND_SLACK = 1.02
VMEM_LIMIT = 56 * 1024 * 1024


def _dot(a, b):
    return jnp.dot(a, b, preferred_element_type=F32)


def _params(n_axes, vmem=VMEM_LIMIT):
    return pltpu.CompilerParams(dimension_semantics=("arbitrary",) * n_axes, vmem_limit_bytes=vmem)


def _rms_rows(x, g):
    ms = jnp.mean(x * x, axis=-1, keepdims=True)
    return x * lax.rsqrt(ms + RMS_EPS) * g


def _split3(x):
    hi = x.astype(BF16)
    r = x - hi.astype(F32)
    mid = r.astype(BF16)
    lo = (r - mid.astype(F32)).astype(BF16)
    return hi, mid, lo


def _row_groups(tm):
    step = min(ROW_GROUP, tm)
    return [slice(r, r + step) for r in range(0, tm, step)]


def _project(h, rows, w_ref, o_ref, col_tiles):
    for j in col_tiles:
        cols = slice(j * PROJ_COL_TILE, (j + 1) * PROJ_COL_TILE)
        o_ref[0, rows, cols] = _dot(h, w_ref[:, cols]).astype(o_ref.dtype)


def _pre_odd_kernel(x_ref, g_ref, w_ref, o_ref):
    groups = _row_groups(x_ref.shape[1])
    hs = [_rms_rows(x_ref[0, rows, :], g_ref[...]).astype(BF16) for rows in groups]
    for rows, h in zip(groups, hs):
        _project(h, rows, w_ref, o_ref, range(w_ref.shape[1] // PROJ_COL_TILE))


def _pre_even_kernel(x_ref, g_ref, w_ref, wfg_ref, fb_ref, ltri_ref, o_ref, c_ref, carry_ref):
    @pl.when(pl.program_id(1) == 0)
    def _():
        carry_ref[...] = jnp.zeros_like(carry_ref)

    groups = _row_groups(x_ref.shape[1])
    n_tiles = w_ref.shape[1] // PROJ_COL_TILE
    tc = ltri_ref.shape[0]
    assert groups[0].stop % tc == 0
    hs = [_rms_rows(x_ref[0, rows, :], g_ref[...]).astype(BF16) for rows in groups]
    pieces = []
    for rows, h in zip(groups, hs):
        fg = _dot(h, wfg_ref[...]) + fb_ref[...]
        logf = jnp.minimum(fg, 0.0) - jnp.log1p(jnp.exp(-jnp.abs(fg)))
        pieces.append([_split3(logf[r:r + tc]) for r in range(0, logf.shape[0], tc)])
    for rows, h in zip(groups, hs):
        _project(h, rows, w_ref, o_ref, range(n_tiles // 2))
    carry = carry_ref[...]
    for rows, chunks in zip(groups, pieces):
        for ci, (hi, mid, lo) in enumerate(chunks):
            cs = _dot(ltri_ref[...], hi) + _dot(ltri_ref[...], mid) + _dot(ltri_ref[...], lo) + carry
            c_ref[0, rows.start + ci * tc:rows.start + (ci + 1) * tc, :] = cs
            carry = cs[tc - 1:tc, :]
    carry_ref[...] = carry
    for rows, h in zip(groups, hs):
        _project(h, rows, w_ref, o_ref, range(n_tiles // 2, n_tiles))


def _pre_call(x, g, w_stack, idx, n, extra=None):
    b, s, d = x.shape
    w = w_stack
    tm = min(PRE_TOKEN_TILE, s)
    grid = (b, s // tm)
    x_spec = pl.BlockSpec((1, tm, d), lambda i, j: (i, j, 0))
    g_spec = pl.BlockSpec((1, d), lambda i, j: (0, 0))
    w_spec = pl.BlockSpec((None, d, n), lambda i, j: (idx, 0, 0))
    o_spec = pl.BlockSpec((1, tm, n), lambda i, j: (i, j, 0))
    o_shape = jax.ShapeDtypeStruct((b, s, n), BF16)
    if extra is None:
        return pl.pallas_call(
            _pre_odd_kernel, grid=grid, in_specs=[x_spec, g_spec, w_spec], out_specs=o_spec,
            out_shape=o_shape, compiler_params=_params(2), name="pre_odd")(x, g, w)
    wfg, fb = extra
    tc = min(K_TILE, tm)
    ltri = (lax.broadcasted_iota(jnp.int32, (tc, tc), 0) >= lax.broadcasted_iota(jnp.int32, (tc, tc), 1)).astype(BF16)
    return pl.pallas_call(
        _pre_even_kernel, grid=grid,
        in_specs=[x_spec, g_spec, w_spec,
                  pl.BlockSpec((d, LANES), lambda i, j: (0, 0)),
                  pl.BlockSpec((1, LANES), lambda i, j: (0, 0)),
                  pl.BlockSpec((tc, tc), lambda i, j: (0, 0))],
        out_specs=[o_spec, pl.BlockSpec((1, tm, LANES), lambda i, j: (i, j, 0))],
        out_shape=[o_shape, jax.ShapeDtypeStruct((b, s, LANES), F32)],
        scratch_shapes=[pltpu.VMEM((1, LANES), F32)],
        compiler_params=_params(2), name="pre_even")(x, g, w, wfg, fb, ltri)


def _key_tile(s):
    tk = min(K_TILE, s)
    assert s % tk == 0 and tk % LANES == 0
    return tk


def _pipeline(items, stages, lag=1):
    carried = [{} for _ in stages]
    for step in range(len(items) + lag * (len(stages) - 1)):
        for i, stage in enumerate(stages):
            idx = step - i * lag
            if 0 <= idx < len(items):
                arg = carried[i - 1].pop(idx) if i else None
                carried[i][idx] = stage(items[idx], arg)


def _descending(n):
    return list(reversed(range(n)))


def _store_split_qt(qn, qat_ref, qbt_ref, col0):
    row = lax.broadcasted_iota(jnp.int32, (LANES, LANES), 0)
    for c in range(qn.shape[0] // LANES):
        cols = slice(col0 + c * LANES, col0 + (c + 1) * LANES)
        qt = qn[c * LANES:(c + 1) * LANES].T
        qat_ref[:, cols] = jnp.where(row < HEAD_DIM, qt, 0.0).astype(BF16)
        qbt_ref[:, cols] = jnp.where(row >= HEAD_DIM, qt, 0.0).astype(BF16)


def _group_rms(x, bd, g):
    ss = _dot((x * x).astype(BF16), bd)
    return x * lax.rsqrt(ss * (1.0 / HEAD_DIM) + RMS_EPS) * g


def _softmax_probs(d, zs, m, tk):
    out = []
    for cb in range(zs[0].shape[1] // tk):
        a = d + cb
        for i, z in enumerate(zs):
            zb = z[:, cb * tk:(cb + 1) * tk]
            m_cur = jnp.max(zb, axis=0, keepdims=True)
            if a == d:
                out.append((a, i, jnp.exp2(zb - m_cur).astype(BF16), None))
                m[i][a] = m_cur
            else:
                m_new = jnp.maximum(m[i][a], m_cur)
                out.append((a, i, jnp.exp2(zb - m_new).astype(BF16), jnp.exp2(m[i][a] - m_new)))
                m[i][a] = m_new
    return out


def _accumulate(probs, acc_refs, vts, tk):
    for a, i, p, alpha in probs:
        cols = slice(a * tk, (a + 1) * tk)
        if alpha is None:
            acc_refs[i][:, cols] = _dot(vts[i], p)
        else:
            acc_refs[i][:, cols] = acc_refs[i][:, cols] * alpha + _dot(vts[i], p)


def _with_ones(vt):
    return jnp.concatenate([vt.astype(BF16), jnp.ones((ONES_ROWS, vt.shape[1]), BF16)], axis=0)


def _sb_kernel(q_ref, k_ref, v_ref, u_ref, o_ref, qat_ref, qbt_ref, acc_ref, r_ref):
    s = q_ref.shape[1]
    tk = u_ref.shape[0]
    nblk = s // tk
    strict = lax.broadcasted_iota(jnp.int32, (tk, tk), 0) < lax.broadcasted_iota(jnp.int32, (tk, tk), 1)
    r = [[None] * nblk for _ in range(2)]

    def suffix_sum(z, tri):
        neg_abs = lax.bitcast_convert_type(lax.bitcast_convert_type(z, jnp.uint32) | SIGN_BIT, F32)
        sp = jnp.maximum(z, 0.0) + LOG2E * jnp.log(1.0 + jnp.exp2(neg_abs))
        if tri:
            sp = jnp.where(strict, sp, 0.0)
        cum = _dot(u_ref[...], sp.astype(BF16))
        return z - sp, cum, cum[0:1, :] + sp[0:1, :]

    def queries(d, _):
        qs = q_ref[0, d * tk:(d + 1) * tk, :].astype(F32) * (HEAD_DIM ** -0.5 * LOG2E)
        _store_split_qt(qs, qat_ref, qbt_ref, d * tk)

    def logits(d, _):
        kblk = k_ref[0, d * tk:(d + 1) * tk, :]
        cols = slice(d * tk, min(d + SB_NEAR_TILES, nblk) * tk)
        return [_dot(kblk, qt_ref[:, cols]) for qt_ref in (qat_ref, qbt_ref)]

    def suffix_sums(d, zs):
        chains = []
        for cb in range(zs[0].shape[1] // tk):
            for sub in range(2):
                chains.append((cb, sub) + suffix_sum(zs[sub][:, cb * tk:(cb + 1) * tk], cb == 0))
        return chains

    def values(d, chains):
        vt = v_ref[0, d * tk:(d + 1) * tk, :].astype(F32).T.astype(BF16)
        for cb, sub, log_beta, cum, total in chains:
            a = d + cb
            rows = slice(sub * HEAD_DIM, (sub + 1) * HEAD_DIM)
            cols = slice(a * tk, (a + 1) * tk)
            if cb == 0:
                w = jnp.where(strict, jnp.exp2(log_beta - cum), 0.0)
                acc_ref[rows, cols] = _dot(vt[rows], w.astype(BF16))
                r[sub][a] = total
            else:
                w = jnp.exp2(log_beta - cum - r[sub][a])
                acc_ref[rows, cols] += _dot(vt[rows], w.astype(BF16))
                r[sub][a] = r[sub][a] + total

    _pipeline(_descending(nblk), (queries, logits, suffix_sums, values))

    def far_tile(d):
        kblk = k_ref[0, d * tk:(d + 1) * tk, :]
        vt = v_ref[0, d * tk:(d + 1) * tk, :].astype(F32).T.astype(BF16)
        for a in range(d + SB_NEAR_TILES, nblk):
            cols = slice(a * tk, (a + 1) * tk)
            for sub, qt_ref in enumerate((qat_ref, qbt_ref)):
                rows = slice(sub * HEAD_DIM, (sub + 1) * HEAD_DIM)
                log_beta, cum, total = suffix_sum(_dot(kblk, qt_ref[:, cols]), False)
                r_old = r_ref[sub:sub + 1, cols]
                acc_ref[rows, cols] += _dot(vt[rows], jnp.exp2(log_beta - cum - r_old).astype(BF16))
                r_ref[sub:sub + 1, cols] = r_old + total

    if nblk > SB_NEAR_TILES:
        for sub in range(2):
            r_ref[sub:sub + 1, :] = jnp.concatenate(r[sub], axis=1)
        @pl.when(jnp.min(r_ref[0:2, SB_NEAR_TILES * tk:]) < SB_ZERO_BITS)
        def _():
            for d in reversed(range(nblk - SB_NEAR_TILES)):
                @pl.when(jnp.min(r_ref[0:2, (d + SB_NEAR_TILES) * tk:]) < SB_ZERO_BITS)
                def _(d=d):
                    far_tile(d)

    for c in range(s // LANES):
        cols = slice(c * LANES, (c + 1) * LANES)
        o_ref[0, cols, :] = acc_ref[:, cols].T.astype(o_ref.dtype)


def _sb_call(proj, n_pairs, q_col, k_col, v_col):
    b, s, _ = proj.shape
    tk = min(SB_K_TILE, s)
    assert s % tk == 0 and tk % LANES == 0
    u = (lax.broadcasted_iota(jnp.int32, (tk, tk), 1) > lax.broadcasted_iota(jnp.int32, (tk, tk), 0)).astype(BF16)
    tok = lambda col: pl.BlockSpec((1, s, LANES), lambda i, p: (i, 0, col + p))
    return pl.pallas_call(
        _sb_kernel, grid=(b, n_pairs),
        in_specs=[tok(q_col), tok(k_col), tok(v_col), pl.BlockSpec((tk, tk), lambda i, p: (0, 0))],
        out_specs=tok(0),
        out_shape=jax.ShapeDtypeStruct((b, s, n_pairs * LANES), BF16),
        scratch_shapes=[pltpu.VMEM((LANES, s), BF16), pltpu.VMEM((LANES, s), BF16), pltpu.VMEM((LANES, s), F32),
                        pltpu.VMEM((8, s), F32)],
        compiler_params=_params(2), name="sb_attn")(proj, proj, proj, u)


def _bounded_flags(gq, gk, extra_bits):
    qk_bits = jnp.max(jnp.abs(gq)) * jnp.max(jnp.abs(gk)) * (HEAD_DIM ** 0.5 * LOG2E * QK_BOUND_SLACK)
    return (qk_bits + extra_bits <= SAFE_LOGIT_BITS).astype(jnp.int32)


def _fox_kernel(bounded_ref, q_ref, k_ref, v_ref, c_ref, gq_ref, gk_ref, bd_ref, o_ref,
                qat_ref, qbt_ref, acca_ref, accb_ref, ct_ref, kn_ref, vta_ref, vtb_ref, cb_ref):
    s = q_ref.shape[1]
    tk = _key_tile(s)
    nblk = s // tk
    p_idx = pl.program_id(1)
    sel_r = lax.broadcasted_iota(jnp.int32, (LANES, 2 * LANES), 0)
    sel_l = lax.broadcasted_iota(jnp.int32, (LANES, 2 * LANES), 1)
    sel = (sel_r == 2 * p_idx + (sel_l >= LANES).astype(jnp.int32)).astype(BF16)
    causal = lax.broadcasted_iota(jnp.int32, (tk, tk), 0) <= lax.broadcasted_iota(jnp.int32, (tk, tk), 1)
    head_row = lax.broadcasted_iota(jnp.int32, (LANES, LANES), 0)
    qt_refs, acc_refs = (qat_ref, qbt_ref), (acca_ref, accb_ref)

    def prepare_queries(a):
        qn = _group_rms(q_ref[0, a * tk:(a + 1) * tk, :].astype(F32), bd_ref[...], gq_ref[...])
        _store_split_qt(qn * (HEAD_DIM ** -0.5 * LOG2E), qat_ref, qbt_ref, a * tk)

    def normed_keys(d):
        return _group_rms(k_ref[0, d * tk:(d + 1) * tk, :].astype(F32), bd_ref[...], gk_ref[...]).astype(BF16)

    def key_decay(d):
        hi, mid, lo = _split3(c_ref[0, d * tk:(d + 1) * tk, :] * LOG2E)
        return _dot(hi, sel) + _dot(mid, sel) + _dot(lo, sel)

    def values_t(d):
        vt = v_ref[0, d * tk:(d + 1) * tk, :].astype(F32).T
        return _with_ones(vt[:HEAD_DIM]), _with_ones(vt[HEAD_DIM:])

    def finalize(a):
        for c in range(a * tk // LANES, (a + 1) * tk // LANES):
            cols = slice(c * LANES, (c + 1) * LANES)
            oa = acca_ref[0:HEAD_DIM, cols] / acca_ref[HEAD_DIM:HEAD_DIM + 1, cols]
            ob = accb_ref[0:HEAD_DIM, cols] / accb_ref[HEAD_DIM:HEAD_DIM + 1, cols]
            o_ref[0, cols, :] = jnp.concatenate([oa, ob], axis=0).T.astype(o_ref.dtype)

    def bounded_sweep():
        def prepare(item, _):
            a, d = item
            if d == a:
                prepare_queries(a)
                for c in range(a * tk // LANES, (a + 1) * tk // LANES):
                    cols = slice(c * LANES, (c + 1) * LANES)
                    ct = (c_ref[0, cols, :] * LOG2E).T
                    for sub in range(2):
                        ct_ref[sub:sub + 1, cols] = jnp.sum(jnp.where(head_row == 2 * p_idx + sub, ct, 0.0),
                                                            axis=0, keepdims=True)
                kn_ref[a * tk:(a + 1) * tk, :] = normed_keys(a)
                cb_ref[a] = key_decay(a)
                vta_ref[a], vtb_ref[a] = values_t(a)

        def logits(item, _):
            a, d = item
            cols = slice(a * tk, (a + 1) * tk)
            kn = kn_ref[d * tk:(d + 1) * tk, :]
            zs = []
            for sub, qt_ref in enumerate(qt_refs):
                decay = cb_ref[d, :, sub * LANES:(sub + 1) * LANES]
                z = (_dot(kn, qt_ref[:, cols]) - jnp.concatenate([decay] * (tk // LANES), axis=1)
                     + ct_ref[sub:sub + 1, cols])
                zs.append(jnp.where(causal, z, NEG_BIG) if d == a else z)
            return zs

        def probs(item, zs):
            return [jnp.exp2(z).astype(BF16) for z in zs]

        def values(item, ps):
            a, d = item
            cols = slice(a * tk, (a + 1) * tk)
            for acc_ref, vt_ref, p in zip(acc_refs, (vta_ref, vtb_ref), ps):
                if d == a:
                    acc_ref[:, cols] = _dot(vt_ref[d], p)
                else:
                    acc_ref[:, cols] += _dot(vt_ref[d], p)
            if d == 0:
                finalize(a)

        items = [(a, d) for a in range(nblk) for d in _descending(a + 1)]
        _pipeline(items, (prepare, logits, probs, values))

    def running_max_sweep():
        m = [[None] * nblk for _ in range(2)]

        def queries(d, _):
            prepare_queries(d)

        def logits(d, _):
            kn = normed_keys(d)
            cb = key_decay(d)
            zs = []
            for sub, qt_ref in enumerate(qt_refs):
                decay = cb[:, sub * LANES:(sub + 1) * LANES]
                z = _dot(kn, qt_ref[:, d * tk:]) - jnp.concatenate([decay] * ((s - d * tk) // LANES), axis=1)
                tri = jnp.where(causal, z[:, :tk], NEG_BIG)
                zs.append(tri if d == nblk - 1 else jnp.concatenate([tri, z[:, tk:]], axis=1))
            return zs

        def probs(d, zs):
            return _softmax_probs(d, zs, m, tk)

        def values(d, ps):
            _accumulate(ps, acc_refs, values_t(d), tk)

        _pipeline(_descending(nblk), (queries, logits, probs, values))
        for a in range(nblk):
            finalize(a)

    is_bounded = bounded_ref[0] > 0
    pl.when(is_bounded)(bounded_sweep)
    pl.when(jnp.logical_not(is_bounded))(running_max_sweep)


def _block_diag_ones():
    blk = lax.broadcasted_iota(jnp.int32, (LANES, LANES), 0) // HEAD_DIM
    return (blk == lax.broadcasted_iota(jnp.int32, (LANES, LANES), 1) // HEAD_DIM).astype(BF16)


def _fox_call(proj, c, gq, gk, n_pairs, q_col, k_col, v_col):
    b, s, _ = proj.shape
    tk = _key_tile(s)
    tok = lambda col: pl.BlockSpec((1, s, LANES), lambda i, p: (i, 0, col + p))
    const = lambda shape: pl.BlockSpec(shape, lambda i, p: (0,) * len(shape))
    rows_aug = HEAD_DIM + ONES_ROWS
    return pl.pallas_call(
        _fox_kernel, grid=(b, n_pairs),
        in_specs=[pl.BlockSpec(memory_space=pltpu.SMEM),
                  tok(q_col), tok(k_col), tok(v_col), pl.BlockSpec((1, s, LANES), lambda i, p: (i, 0, 0)),
                  const((1, LANES)), const((1, LANES)), const((LANES, LANES))],
        out_specs=tok(0),
        out_shape=jax.ShapeDtypeStruct((b, s, n_pairs * LANES), BF16),
        scratch_shapes=[pltpu.VMEM((LANES, s), BF16), pltpu.VMEM((LANES, s), BF16),
                        pltpu.VMEM((rows_aug, s), F32), pltpu.VMEM((rows_aug, s), F32), pltpu.VMEM((8, s), F32),
                        pltpu.VMEM((s, LANES), BF16),
                        pltpu.VMEM((s // tk, rows_aug, tk), BF16), pltpu.VMEM((s // tk, rows_aug, tk), BF16),
                        pltpu.VMEM((s // tk, tk, 2 * LANES), F32)],
        compiler_params=_params(2), name="fox_attn")(
            _bounded_flags(gq, gk, jnp.zeros((1,), F32)), proj, proj, proj, c, gq, gk, _block_diag_ones())


def _diff_kernel(bounded_ref, q_ref, k_ref, v_ref, bias_ref, gq_ref, gk_ref, bd_ref, lam_ref, sg_ref, o_ref,
                 q1t_ref, q2t_ref, acc1_ref, acc2_ref, kn_ref, vt_ref, *, lam_init):
    s = q_ref.shape[1]
    tk = bias_ref.shape[2]
    nblk = s // tk
    dv = v_ref.shape[2]
    causal = lax.broadcasted_iota(jnp.int32, (tk, tk), 0) <= lax.broadcasted_iota(jnp.int32, (tk, tk), 1)
    lam = (jnp.exp(jnp.sum(lam_ref[0:1, :] * lam_ref[1:2, :], axis=-1, keepdims=True))
           - jnp.exp(jnp.sum(lam_ref[2:3, :] * lam_ref[3:4, :], axis=-1, keepdims=True)) + lam_init)
    out_gain = sg_ref[...] * (1.0 - lam_init)

    def prepare_queries(a):
        qn = _group_rms(q_ref[0, a * tk:(a + 1) * tk, :].astype(F32), bd_ref[...], gq_ref[...])
        _store_split_qt(qn * (HEAD_DIM ** -0.5 * LOG2E), q1t_ref, q2t_ref, a * tk)

    def normed_keys(d):
        return _group_rms(k_ref[0, d * tk:(d + 1) * tk, :].astype(F32), bd_ref[...], gk_ref[...]).astype(BF16)

    def values_t(d):
        return _with_ones(v_ref[0, d * tk:(d + 1) * tk, :].astype(F32).T)

    def biased(z, a, d):
        if d == a:
            return jnp.where(causal, z + bias_ref[0, 0], NEG_BIG)
        return z + bias_ref[0, 1] if d == a - 1 else z

    def finalize(a):
        for c in range(a * tk // LANES, (a + 1) * tk // LANES):
            cols = slice(c * LANES, (c + 1) * LANES)
            o1 = acc1_ref[0:dv, cols] / acc1_ref[dv:dv + 1, cols]
            o2 = acc2_ref[0:dv, cols] / acc2_ref[dv:dv + 1, cols]
            o = o1 - lam * o2
            o = o * lax.rsqrt(jnp.mean(o * o, axis=0, keepdims=True) + RMS_EPS)
            o_ref[0, cols, :] = (o.T * out_gain).astype(o_ref.dtype)

    def bounded_sweep():
        def prepare(item, _):
            a, d = item
            if d == a:
                prepare_queries(a)
                kn_ref[a * tk:(a + 1) * tk, :] = normed_keys(a)
                vt_ref[a] = values_t(a)

        def logits(item, _):
            a, d = item
            kn = kn_ref[d * tk:(d + 1) * tk, :]
            return [biased(_dot(kn, qt_ref[:, a * tk:(a + 1) * tk]), a, d) for qt_ref in (q1t_ref, q2t_ref)]

        def probs(item, zs):
            return [jnp.exp2(z).astype(BF16) for z in zs]

        def values(item, ps):
            a, d = item
            cols = slice(a * tk, (a + 1) * tk)
            for acc_ref, p in zip((acc1_ref, acc2_ref), ps):
                if d == a:
                    acc_ref[:, cols] = _dot(vt_ref[d], p)
                else:
                    acc_ref[:, cols] += _dot(vt_ref[d], p)
            if d == 0:
                finalize(a)

        items = [(a, d) for a in range(nblk) for d in _descending(a + 1)]
        _pipeline(items, (prepare, logits, probs, values), lag=3)

    def running_max_sweep():
        m = [[None] * nblk for _ in range(2)]

        def queries(d, _):
            prepare_queries(d)

        def logits(d, _):
            kn = normed_keys(d)
            zs = []
            for qt_ref in (q1t_ref, q2t_ref):
                z = _dot(kn, qt_ref[:, d * tk:])
                zs.append(jnp.concatenate([biased(z[:, (a - d) * tk:(a - d + 1) * tk], a, d) for a in range(d, min(d + 2, nblk))]
                                          + ([z[:, 2 * tk:]] if d < nblk - 2 else []), axis=1))
            return zs

        def probs(d, zs):
            return _softmax_probs(d, zs, m, tk)

        def values(d, ps):
            vt = values_t(d)
            _accumulate(ps, (acc1_ref, acc2_ref), (vt, vt), tk)

        _pipeline(_descending(nblk), (queries, logits, probs, values))
        for a in range(nblk):
            finalize(a)

    is_bounded = bounded_ref[pl.program_id(1)] > 0
    pl.when(is_bounded)(bounded_sweep)
    pl.when(jnp.logical_not(is_bounded))(running_max_sweep)


def _diff_call(proj, bias, gq, gk, lam_rows, subln_g, n_heads, lam_init):
    b, s, _ = proj.shape
    tk = _key_tile(s)
    tok = lambda col: pl.BlockSpec((1, s, LANES), lambda i, p: (i, 0, col + p))
    const = lambda shape: pl.BlockSpec(shape, lambda i, p: (0,) * len(shape))
    dv = 2 * HEAD_DIM
    assert dv == LANES
    rows_aug = dv + ONES_ROWS
    return pl.pallas_call(
        functools.partial(_diff_kernel, lam_init=lam_init), grid=(b, n_heads),
        in_specs=[pl.BlockSpec(memory_space=pltpu.SMEM), tok(0), tok(n_heads), tok(2 * n_heads),
                  pl.BlockSpec((1, 2, tk, tk), lambda i, p: (p, 0, 0, 0)),
                  const((1, LANES)), const((1, LANES)), const((LANES, LANES)), const((8, LANES)), const((1, dv))],
        out_specs=tok(0),
        out_shape=jax.ShapeDtypeStruct((b, s, n_heads * dv), BF16),
        scratch_shapes=[pltpu.VMEM((LANES, s), BF16), pltpu.VMEM((LANES, s), BF16),
                        pltpu.VMEM((rows_aug, s), F32), pltpu.VMEM((rows_aug, s), F32),
                        pltpu.VMEM((s, LANES), BF16), pltpu.VMEM((s // tk, rows_aug, tk), BF16)],
        compiler_params=_params(2), name="diff_attn")(
            _bounded_flags(gq, gk, jnp.max(jnp.abs(bias), axis=(1, 2, 3))), proj, proj, proj, bias, gq, gk,
            _block_diag_ones(), lam_rows, subln_g)


def _t5_bucket(dist):
    max_exact = N_BUCKETS // 2
    nf = jnp.maximum(dist, 1).astype(F32)
    large = max_exact + (jnp.log(nf / max_exact) / math.log(MAX_DISTANCE / max_exact)
                         * (N_BUCKETS - max_exact)).astype(jnp.int32)
    large = jnp.minimum(large, N_BUCKETS - 1)
    return jnp.where(dist < max_exact, dist, large)


def _bias_tiles(rel_bias, t):
    assert t >= MAX_DISTANCE
    table = rel_bias.astype(F32)
    krow = lax.broadcasted_iota(jnp.int32, (2, t, t), 1)
    qcol = lax.broadcasted_iota(jnp.int32, (2, t, t), 2)
    dist = qcol - krow + t * lax.broadcasted_iota(jnp.int32, (2, t, t), 0)
    onehot = (_t5_bucket(jnp.maximum(dist, 0))[..., None] == jnp.arange(N_BUCKETS)).astype(F32)
    return jnp.einsum('otkb,bh->hotk', onehot, (table - table[N_BUCKETS - 1]) * LOG2E,
                      precision=lax.Precision.HIGHEST)


def _post_kernel(*refs, n_mix):
    x_ref = refs[0]
    o_refs = refs[1:1 + n_mix]
    wo_ref, g_ref, wg_ref, wu_ref, wd_ref, out_ref, x1_ref, act_ref = refs[1 + n_mix:]
    tm, d = x_ref.shape[1:]
    d_ff = wg_ref.shape[1]
    halves = _row_groups(tm)
    for rows in halves:
        o = jnp.concatenate([o_ref[0, rows, :] for o_ref in o_refs], axis=1) if n_mix > 1 else o_refs[0][0, rows, :]
        for lo in range(0, d, PROJ_COL_TILE):
            cols = slice(lo, lo + PROJ_COL_TILE)
            x1_ref[rows, cols] = x_ref[0, rows, cols] + _dot(o, wo_ref[:, cols])
    for rows in halves:
        h = _rms_rows(x1_ref[rows, :], g_ref[...]).astype(BF16)
        for lo in range(0, d_ff, FFN_COL_TILE):
            cols = slice(lo, min(lo + FFN_COL_TILE, d_ff))
            a = _dot(h, wg_ref[:, cols])
            u = _dot(h, wu_ref[:, cols])
            act_ref[rows, cols] = (a * (1.0 / (1.0 + jnp.exp(-a))) * u).astype(BF16)
    for rows in halves:
        for lo in range(0, d, PROJ_COL_TILE):
            cols = slice(lo, lo + PROJ_COL_TILE)
            out_ref[0, rows, cols] = x1_ref[rows, cols] + _dot(act_ref[rows, :], wd_ref[:, cols])


def _post_call(x, mixes, w_out, g, wg, wu, wd):
    b, s, d = x.shape
    tm = min(TOKEN_TILE, s)
    n_mix = len(mixes)
    tok = lambda width: pl.BlockSpec((1, tm, width), lambda i, j: (i, j, 0))
    res = lambda shape: pl.BlockSpec(shape, lambda i, j: (0, 0), pipeline_mode=pl.Buffered(1))
    return pl.pallas_call(
        functools.partial(_post_kernel, n_mix=n_mix), grid=(b, s // tm),
        in_specs=([tok(d)] + [tok(m.shape[2]) for m in mixes]
                  + [res(w_out.shape), res((1, d)), res(wg.shape), res(wu.shape), res(wd.shape)]),
        out_specs=tok(d),
        out_shape=jax.ShapeDtypeStruct((b, s, d), x.dtype),
        scratch_shapes=[pltpu.VMEM((tm, d), F32), pltpu.VMEM((tm, wg.shape[1]), BF16)],
        compiler_params=_params(2), name="post")(x, *mixes, w_out, g, wg, wu, wd)


def _pair_gain(g):
    return jnp.concatenate([g, g]).astype(F32)[None, :]


def kernel(x, attn_norm_g, ffn_norm_g, even_w_in, fox_forget_b, fox_q_norm_g, fox_k_norm_g, even_w_out, diff_w_in, diff_q_norm_g, diff_k_norm_g, diff_lambda_q1, diff_lambda_k1, diff_lambda_q2, diff_lambda_k2, diff_subln_g, diff_w_out, rel_bias, ffn_w_gate, ffn_w_up, ffn_w_down):
    b, s, d = x.shape
    depth = attn_norm_g.shape[0]
    n_sb = d // (2 * HEAD_DIM)
    n_fox = d // (2 * HEAD_DIM)
    n_diff = d // (2 * HEAD_DIM)
    sb_w = n_sb * HEAD_DIM
    fox_w = n_fox * HEAD_DIM
    main_w = 3 * sb_w + 3 * fox_w
    assert sb_w % LANES == 0 and fox_w % LANES == 0 and n_fox <= LANES
    bias = _bias_tiles(rel_bias, _key_tile(s))
    even_w_in_bf, diff_w_in_bf = even_w_in.astype(BF16), diff_w_in.astype(BF16)

    for layer in range(depth):
        g_attn = attn_norm_g[layer][None, :]
        if layer % 2 == 0:
            e = layer // 2
            wfg = jnp.pad(even_w_in_bf[e][:, main_w:], ((0, 0), (0, LANES - n_fox)))
            fb = jnp.pad(fox_forget_b[e].astype(F32), (0, LANES - n_fox))[None, :]
            proj, c = _pre_call(x, g_attn, even_w_in_bf, e, main_w, (wfg, fb))
            sbc, fxc = sb_w // LANES, fox_w // LANES
            o_a = _sb_call(proj, sbc, 0, sbc, 2 * sbc)
            o_b = _fox_call(proj, c, _pair_gain(fox_q_norm_g[e]), _pair_gain(fox_k_norm_g[e]),
                            fxc, 3 * sbc, 3 * sbc + fxc, 3 * sbc + 2 * fxc)
            mixes, w_out = [o_a, o_b], even_w_out[e].astype(BF16)
        else:
            o = layer // 2
            proj = _pre_call(x, g_attn, diff_w_in_bf, o, diff_w_in_bf.shape[2])
            lam_rows = jnp.stack([diff_lambda_q1[o], diff_lambda_k1[o], diff_lambda_q2[o], diff_lambda_k2[o]])
            lam_rows = jnp.pad(lam_rows.astype(F32), ((0, 4), (0, LANES - HEAD_DIM)))
            lam_init = 0.8 - 0.6 * math.exp(-0.3 * layer)
            mix = _diff_call(proj, bias, _pair_gain(diff_q_norm_g[o]), _pair_gain(diff_k_norm_g[o]),
                             lam_rows, diff_subln_g[o].astype(F32)[None, :], n_diff, lam_init)
            mixes, w_out = [mix], diff_w_out[o].astype(BF16)
        x = _post_call(x, mixes, w_out, ffn_norm_g[layer][None, :], ffn_w_gate[layer].astype(BF16),
                       ffn_w_up[layer].astype(BF16), ffn_w_down[layer].astype(BF16))
    return x
```

```python
import functools
import math

import jax
import jax.numpy as jnp
import numpy as np
from jax import lax
from jax.experimental import pallas as pl
from jax.experimental.pallas import tpu as pltpu

F32 = jnp.float32
BF16 = jnp.bfloat16

HEAD_DIM = 64
N_BUCKETS = 32
MAX_DISTANCE = 128
RMS_EPS = 1e-6
LOG2E = 1.4426950408889634
NEG_BIG = -1e30
SIGN_BIT = np.uint32(0x80000000)

LANES = 128
K_TILE = 256
TOKEN_TILE = 1024
PRE_TOKEN_TILE = 1024
PROJ_COL_TILE = 512
FFN_COL_TILE = 512
ROW_GROUP = 256
ONES_ROWS = 16
SB_K_TILE = 128
SB_NEAR_TILES = 3
SB_ZERO_BITS = 152.0
SAFE_LOGIT_BITS = 30.0
QK_BOUND_SLACK = 1.02
VMEM_LIMIT = 56 * 1024 * 1024


def _dot(a, b):
    return jnp.dot(a, b, preferred_element_type=F32)


def _params(n_axes, vmem=VMEM_LIMIT):
    return pltpu.CompilerParams(dimension_semantics=("arbitrary",) * n_axes, vmem_limit_bytes=vmem)


def _rms_rows(x, g):
    ms = jnp.mean(x * x, axis=-1, keepdims=True)
    return x * lax.rsqrt(ms + RMS_EPS) * g


def _split3(x):
    hi = x.astype(BF16)
    r = x - hi.astype(F32)
    mid = r.astype(BF16)
    lo = (r - mid.astype(F32)).astype(BF16)
    return hi, mid, lo


def _row_groups(tm):
    step = min(ROW_GROUP, tm)
    return [slice(r, r + step) for r in range(0, tm, step)]


def _project(h, rows, w_ref, o_ref, col_tiles):
    for j in col_tiles:
        cols = slice(j * PROJ_COL_TILE, (j + 1) * PROJ_COL_TILE)
        o_ref[0, rows, cols] = _dot(h, w_ref[:, cols]).astype(o_ref.dtype)


def _pre_odd_kernel(x_ref, g_ref, w_ref, o_ref):
    groups = _row_groups(x_ref.shape[1])
    hs = [_rms_rows(x_ref[0, rows, :], g_ref[...]).astype(BF16) for rows in groups]
    for rows, h in zip(groups, hs):
        _project(h, rows, w_ref, o_ref, range(w_ref.shape[1] // PROJ_COL_TILE))


def _pre_even_kernel(x_ref, g_ref, w_ref, wfg_ref, fb_ref, ltri_ref, o_ref, c_ref, carry_ref):
    @pl.when(pl.program_id(1) == 0)
    def _():
        carry_ref[...] = jnp.zeros_like(carry_ref)

    groups = _row_groups(x_ref.shape[1])
    n_tiles = w_ref.shape[1] // PROJ_COL_TILE
    tc = ltri_ref.shape[0]
    assert groups[0].stop % tc == 0
    hs = [_rms_rows(x_ref[0, rows, :], g_ref[...]).astype(BF16) for rows in groups]
    pieces = []
    for rows, h in zip(groups, hs):
        fg = _dot(h, wfg_ref[...]) + fb_ref[...]
        logf = jnp.minimum(fg, 0.0) - jnp.log1p(jnp.exp(-jnp.abs(fg)))
        pieces.append([_split3(logf[r:r + tc]) for r in range(0, logf.shape[0], tc)])
    for rows, h in zip(groups, hs):
        _project(h, rows, w_ref, o_ref, range(n_tiles // 2))
    carry = carry_ref[...]
    for rows, chunks in zip(groups, pieces):
        for ci, (hi, mid, lo) in enumerate(chunks):
            cs = _dot(ltri_ref[...], hi) + _dot(ltri_ref[...], mid) + _dot(ltri_ref[...], lo) + carry
            c_ref[0, rows.start + ci * tc:rows.start + (ci + 1) * tc, :] = cs
            carry = cs[tc - 1:tc, :]
    carry_ref[...] = carry
    for rows, h in zip(groups, hs):
        _project(h, rows, w_ref, o_ref, range(n_tiles // 2, n_tiles))


def _pre_call(x, g, w_stack, idx, n, extra=None):
    b, s, d = x.shape
    w = w_stack
    tm = min(PRE_TOKEN_TILE, s)
    grid = (b, s // tm)
    x_spec = pl.BlockSpec((1, tm, d), lambda i, j: (i, j, 0))
    g_spec = pl.BlockSpec((1, d), lambda i, j: (0, 0))
    w_spec = pl.BlockSpec((None, d, n), lambda i, j: (idx, 0, 0))
    o_spec = pl.BlockSpec((1, tm, n), lambda i, j: (i, j, 0))
    o_shape = jax.ShapeDtypeStruct((b, s, n), BF16)
    if extra is None:
        return pl.pallas_call(
            _pre_odd_kernel, grid=grid, in_specs=[x_spec, g_spec, w_spec], out_specs=o_spec,
            out_shape=o_shape, compiler_params=_params(2), name="pre_odd")(x, g, w)
    wfg, fb = extra
    tc = min(K_TILE, tm)
    ltri = (lax.broadcasted_iota(jnp.int32, (tc, tc), 0) >= lax.broadcasted_iota(jnp.int32, (tc, tc), 1)).astype(BF16)
    return pl.pallas_call(
        _pre_even_kernel, grid=grid,
        in_specs=[x_spec, g_spec, w_spec,
                  pl.BlockSpec((d, LANES), lambda i, j: (0, 0)),
                  pl.BlockSpec((1, LANES), lambda i, j: (0, 0)),
                  pl.BlockSpec((tc, tc), lambda i, j: (0, 0))],
        out_specs=[o_spec, pl.BlockSpec((1, tm, LANES), lambda i, j: (i, j, 0))],
        out_shape=[o_shape, jax.ShapeDtypeStruct((b, s, LANES), F32)],
        scratch_shapes=[pltpu.VMEM((1, LANES), F32)],
        compiler_params=_params(2), name="pre_even")(x, g, w, wfg, fb, ltri)


def _key_tile(s):
    tk = min(K_TILE, s)
    assert s % tk == 0 and tk % LANES == 0
    return tk


def _pipeline(items, stages, lag=1):
    carried = [{} for _ in stages]
    for step in range(len(items) + lag * (len(stages) - 1)):
        for i, stage in enumerate(stages):
            idx = step - i * lag
            if 0 <= idx < len(items):
                arg = carried[i - 1].pop(idx) if i else None
                carried[i][idx] = stage(items[idx], arg)


def _descending(n):
    return list(reversed(range(n)))


def _store_split_qt(qn, qat_ref, qbt_ref, col0):
    row = lax.broadcasted_iota(jnp.int32, (LANES, LANES), 0)
    for c in range(qn.shape[0] // LANES):
        cols = slice(col0 + c * LANES, col0 + (c + 1) * LANES)
        qt = qn[c * LANES:(c + 1) * LANES].T
        qat_ref[:, cols] = jnp.where(row < HEAD_DIM, qt, 0.0).astype(BF16)
        qbt_ref[:, cols] = jnp.where(row >= HEAD_DIM, qt, 0.0).astype(BF16)


def _group_rms(x, bd, g):
    ss = _dot((x * x).astype(BF16), bd)
    return x * lax.rsqrt(ss * (1.0 / HEAD_DIM) + RMS_EPS) * g


def _softmax_probs(d, zs, m, tk):
    out = []
    for cb in range(zs[0].shape[1] // tk):
        a = d + cb
        for i, z in enumerate(zs):
            zb = z[:, cb * tk:(cb + 1) * tk]
            m_cur = jnp.max(zb, axis=0, keepdims=True)
            if a == d:
                out.append((a, i, jnp.exp2(zb - m_cur).astype(BF16), None))
                m[i][a] = m_cur
            else:
                m_new = jnp.maximum(m[i][a], m_cur)
                out.append((a, i, jnp.exp2(zb - m_new).astype(BF16), jnp.exp2(m[i][a] - m_new)))
                m[i][a] = m_new
    return out


def _accumulate(probs, acc_refs, vts, tk):
    for a, i, p, alpha in probs:
        cols = slice(a * tk, (a + 1) * tk)
        if alpha is None:
            acc_refs[i][:, cols] = _dot(vts[i], p)
        else:
            acc_refs[i][:, cols] = acc_refs[i][:, cols] * alpha + _dot(vts[i], p)


def _with_ones(vt):
    return jnp.concatenate([vt.astype(BF16), jnp.ones((ONES_ROWS, vt.shape[1]), BF16)], axis=0)


def _sb_kernel(q_ref, k_ref, v_ref, u_ref, o_ref, qat_ref, qbt_ref, acc_ref, r_ref):
    s = q_ref.shape[1]
    tk = u_ref.shape[0]
    nblk = s // tk
    strict = lax.broadcasted_iota(jnp.int32, (tk, tk), 0) < lax.broadcasted_iota(jnp.int32, (tk, tk), 1)
    r = [[None] * nblk for _ in range(2)]

    def suffix_sum(z, tri):
        neg_abs = lax.bitcast_convert_type(lax.bitcast_convert_type(z, jnp.uint32) | SIGN_BIT, F32)
        sp = jnp.maximum(z, 0.0) + LOG2E * jnp.log(1.0 + jnp.exp2(neg_abs))
        if tri:
            sp = jnp.where(strict, sp, 0.0)
        cum = _dot(u_ref[...], sp.astype(BF16))
        return z - sp, cum, cum[0:1, :] + sp[0:1, :]

    def queries(d, _):
        qs = q_ref[0, d * tk:(d + 1) * tk, :].astype(F32) * (HEAD_DIM ** -0.5 * LOG2E)
        _store_split_qt(qs, qat_ref, qbt_ref, d * tk)

    def logits(d, _):
        kblk = k_ref[0, d * tk:(d + 1) * tk, :]
        cols = slice(d * tk, min(d + SB_NEAR_TILES, nblk) * tk)
        return [_dot(kblk, qt_ref[:, cols]) for qt_ref in (qat_ref, qbt_ref)]

    def suffix_sums(d, zs):
        chains = []
        for cb in range(zs[0].shape[1] // tk):
            for sub in range(2):
                chains.append((cb, sub) + suffix_sum(zs[sub][:, cb * tk:(cb + 1) * tk], cb == 0))
        return chains

    def values(d, chains):
        vt = v_ref[0, d * tk:(d + 1) * tk, :].astype(F32).T.astype(BF16)
        for cb, sub, log_beta, cum, total in chains:
            a = d + cb
            rows = slice(sub * HEAD_DIM, (sub + 1) * HEAD_DIM)
            cols = slice(a * tk, (a + 1) * tk)
            if cb == 0:
                w = jnp.where(strict, jnp.exp2(log_beta - cum), 0.0)
                acc_ref[rows, cols] = _dot(vt[rows], w.astype(BF16))
                r[sub][a] = total
            else:
                w = jnp.exp2(log_beta - cum - r[sub][a])
                acc_ref[rows, cols] += _dot(vt[rows], w.astype(BF16))
                r[sub][a] = r[sub][a] + total

    _pipeline(_descending(nblk), (queries, logits, suffix_sums, values))

    def far_tile(d):
        kblk = k_ref[0, d * tk:(d + 1) * tk, :]
        vt = v_ref[0, d * tk:(d + 1) * tk, :].astype(F32).T.astype(BF16)
        for a in range(d + SB_NEAR_TILES, nblk):
            cols = slice(a * tk, (a + 1) * tk)
            for sub, qt_ref in enumerate((qat_ref, qbt_ref)):
                rows = slice(sub * HEAD_DIM, (sub + 1) * HEAD_DIM)
                log_beta, cum, total = suffix_sum(_dot(kblk, qt_ref[:, cols]), False)
                r_old = r_ref[sub:sub + 1, cols]
                acc_ref[rows, cols] += _dot(vt[rows], jnp.exp2(log_beta - cum - r_old).astype(BF16))
                r_ref[sub:sub + 1, cols] = r_old + total

    if nblk > SB_NEAR_TILES:
        for sub in range(2):
            r_ref[sub:sub + 1, :] = jnp.concatenate(r[sub], axis=1)
        @pl.when(jnp.min(r_ref[0:2, SB_NEAR_TILES * tk:]) < SB_ZERO_BITS)
        def _():
            for d in reversed(range(nblk - SB_NEAR_TILES)):
                @pl.when(jnp.min(r_ref[0:2, (d + SB_NEAR_TILES) * tk:]) < SB_ZERO_BITS)
                def _(d=d):
                    far_tile(d)

    for c in range(s // LANES):
        cols = slice(c * LANES, (c + 1) * LANES)
        o_ref[0, cols, :] = acc_ref[:, cols].T.astype(o_ref.dtype)


def _sb_call(proj, n_pairs, q_col, k_col, v_col):
    b, s, _ = proj.shape
    tk = min(SB_K_TILE, s)
    assert s % tk == 0 and tk % LANES == 0
    u = (lax.broadcasted_iota(jnp.int32, (tk, tk), 1) > lax.broadcasted_iota(jnp.int32, (tk, tk), 0)).astype(BF16)
    tok = lambda col: pl.BlockSpec((1, s, LANES), lambda i, p: (i, 0, col + p))
    return pl.pallas_call(
        _sb_kernel, grid=(b, n_pairs),
        in_specs=[tok(q_col), tok(k_col), tok(v_col), pl.BlockSpec((tk, tk), lambda i, p: (0, 0))],
        out_specs=tok(0),
        out_shape=jax.ShapeDtypeStruct((b, s, n_pairs * LANES), BF16),
        scratch_shapes=[pltpu.VMEM((LANES, s), BF16), pltpu.VMEM((LANES, s), BF16), pltpu.VMEM((LANES, s), F32),
                        pltpu.VMEM((8, s), F32)],
        compiler_params=_params(2), name="sb_attn")(proj, proj, proj, u)


def _bounded_flags(gq, gk, extra_bits):
    qk_bits = jnp.max(jnp.abs(gq)) * jnp.max(jnp.abs(gk)) * (HEAD_DIM ** 0.5 * LOG2E * QK_BOUND_SLACK)
    return (qk_bits + extra_bits <= SAFE_LOGIT_BITS).astype(jnp.int32)


def _fox_kernel(bounded_ref, q_ref, k_ref, v_ref, c_ref, gq_ref, gk_ref, bd_ref, o_ref,
                qat_ref, qbt_ref, acca_ref, accb_ref, ct_ref, kn_ref, vta_ref, vtb_ref, cb_ref):
    s = q_ref.shape[1]
    tk = _key_tile(s)
    nblk = s // tk
    p_idx = pl.program_id(1)
    sel_r = lax.broadcasted_iota(jnp.int32, (LANES, 2 * LANES), 0)
    sel_l = lax.broadcasted_iota(jnp.int32, (LANES, 2 * LANES), 1)
    sel = (sel_r == 2 * p_idx + (sel_l >= LANES).astype(jnp.int32)).astype(BF16)
    causal = lax.broadcasted_iota(jnp.int32, (tk, tk), 0) <= lax.broadcasted_iota(jnp.int32, (tk, tk), 1)
    head_row = lax.broadcasted_iota(jnp.int32, (LANES, LANES), 0)
    qt_refs, acc_refs = (qat_ref, qbt_ref), (acca_ref, accb_ref)

    def prepare_queries(a):
        qn = _group_rms(q_ref[0, a * tk:(a + 1) * tk, :].astype(F32), bd_ref[...], gq_ref[...])
        _store_split_qt(qn * (HEAD_DIM ** -0.5 * LOG2E), qat_ref, qbt_ref, a * tk)

    def normed_keys(d):
        return _group_rms(k_ref[0, d * tk:(d + 1) * tk, :].astype(F32), bd_ref[...], gk_ref[...]).astype(BF16)

    def key_decay(d):
        hi, mid, lo = _split3(c_ref[0, d * tk:(d + 1) * tk, :] * LOG2E)
        return _dot(hi, sel) + _dot(mid, sel) + _dot(lo, sel)

    def values_t(d):
        vt = v_ref[0, d * tk:(d + 1) * tk, :].astype(F32).T
        return _with_ones(vt[:HEAD_DIM]), _with_ones(vt[HEAD_DIM:])

    def finalize(a):
        for c in range(a * tk // LANES, (a + 1) * tk // LANES):
            cols = slice(c * LANES, (c + 1) * LANES)
            oa = acca_ref[0:HEAD_DIM, cols] / acca_ref[HEAD_DIM:HEAD_DIM + 1, cols]
            ob = accb_ref[0:HEAD_DIM, cols] / accb_ref[HEAD_DIM:HEAD_DIM + 1, cols]
            o_ref[0, cols, :] = jnp.concatenate([oa, ob], axis=0).T.astype(o_ref.dtype)

    def bounded_sweep():
        def prepare(item, _):
            a, d = item
            if d == a:
                prepare_queries(a)
                for c in range(a * tk // LANES, (a + 1) * tk // LANES):
                    cols = slice(c * LANES, (c + 1) * LANES)
                    ct = (c_ref[0, cols, :] * LOG2E).T
                    for sub in range(2):
                        ct_ref[sub:sub + 1, cols] = jnp.sum(jnp.where(head_row == 2 * p_idx + sub, ct, 0.0),
                                                            axis=0, keepdims=True)
                kn_ref[a * tk:(a + 1) * tk, :] = normed_keys(a)
                cb_ref[a] = key_decay(a)
                vta_ref[a], vtb_ref[a] = values_t(a)

        def logits(item, _):
            a, d = item
            cols = slice(a * tk, (a + 1) * tk)
            kn = kn_ref[d * tk:(d + 1) * tk, :]
            zs = []
            for sub, qt_ref in enumerate(qt_refs):
                decay = cb_ref[d, :, sub * LANES:(sub + 1) * LANES]
                z = (_dot(kn, qt_ref[:, cols]) - jnp.concatenate([decay] * (tk // LANES), axis=1)
                     + ct_ref[sub:sub + 1, cols])
                zs.append(jnp.where(causal, z, NEG_BIG) if d == a else z)
            return zs

        def probs(item, zs):
            return [jnp.exp2(z).astype(BF16) for z in zs]

        def values(item, ps):
            a, d = item
            cols = slice(a * tk, (a + 1) * tk)
            for acc_ref, vt_ref, p in zip(acc_refs, (vta_ref, vtb_ref), ps):
                if d == a:
                    acc_ref[:, cols] = _dot(vt_ref[d], p)
                else:
                    acc_ref[:, cols] += _dot(vt_ref[d], p)
            if d == 0:
                finalize(a)

        items = [(a, d) for a in range(nblk) for d in _descending(a + 1)]
        _pipeline(items, (prepare, logits, probs, values))

    def running_max_sweep():
        m = [[None] * nblk for _ in range(2)]

        def queries(d, _):
            prepare_queries(d)

        def logits(d, _):
            kn = normed_keys(d)
            cb = key_decay(d)
            zs = []
            for sub, qt_ref in enumerate(qt_refs):
                decay = cb[:, sub * LANES:(sub + 1) * LANES]
                z = _dot(kn, qt_ref[:, d * tk:]) - jnp.concatenate([decay] * ((s - d * tk) // LANES), axis=1)
                tri = jnp.where(causal, z[:, :tk], NEG_BIG)
                zs.append(tri if d == nblk - 1 else jnp.concatenate([tri, z[:, tk:]], axis=1))
            return zs

        def probs(d, zs):
            return _softmax_probs(d, zs, m, tk)

        def values(d, ps):
            _accumulate(ps, acc_refs, values_t(d), tk)

        _pipeline(_descending(nblk), (queries, logits, probs, values))
        for a in range(nblk):
            finalize(a)

    is_bounded = bounded_ref[0] > 0
    pl.when(is_bounded)(bounded_sweep)
    pl.when(jnp.logical_not(is_bounded))(running_max_sweep)


def _block_diag_ones():
    blk = lax.broadcasted_iota(jnp.int32, (LANES, LANES), 0) // HEAD_DIM
    return (blk == lax.broadcasted_iota(jnp.int32, (LANES, LANES), 1) // HEAD_DIM).astype(BF16)


def _fox_call(proj, c, gq, gk, n_pairs, q_col, k_col, v_col):
    b, s, _ = proj.shape
    tk = _key_tile(s)
    tok = lambda col: pl.BlockSpec((1, s, LANES), lambda i, p: (i, 0, col + p))
    const = lambda shape: pl.BlockSpec(shape, lambda i, p: (0,) * len(shape))
    rows_aug = HEAD_DIM + ONES_ROWS
    return pl.pallas_call(
        _fox_kernel, grid=(b, n_pairs),
        in_specs=[pl.BlockSpec(memory_space=pltpu.SMEM),
                  tok(q_col), tok(k_col), tok(v_col), pl.BlockSpec((1, s, LANES), lambda i, p: (i, 0, 0)),
                  const((1, LANES)), const((1, LANES)), const((LANES, LANES))],
        out_specs=tok(0),
        out_shape=jax.ShapeDtypeStruct((b, s, n_pairs * LANES), BF16),
        scratch_shapes=[pltpu.VMEM((LANES, s), BF16), pltpu.VMEM((LANES, s), BF16),
                        pltpu.VMEM((rows_aug, s), F32), pltpu.VMEM((rows_aug, s), F32), pltpu.VMEM((8, s), F32),
                        pltpu.VMEM((s, LANES), BF16),
                        pltpu.VMEM((s // tk, rows_aug, tk), BF16), pltpu.VMEM((s // tk, rows_aug, tk), BF16),
                        pltpu.VMEM((s // tk, tk, 2 * LANES), F32)],
        compiler_params=_params(2), name="fox_attn")(
            _bounded_flags(gq, gk, jnp.zeros((1,), F32)), proj, proj, proj, c, gq, gk, _block_diag_ones())


def _diff_kernel(bounded_ref, q_ref, k_ref, v_ref, bias_ref, gq_ref, gk_ref, bd_ref, lam_ref, sg_ref, o_ref,
                 q1t_ref, q2t_ref, acc1_ref, acc2_ref, kn_ref, vt_ref, *, lam_init):
    s = q_ref.shape[1]
    tk = bias_ref.shape[2]
    nblk = s // tk
    dv = v_ref.shape[2]
    causal = lax.broadcasted_iota(jnp.int32, (tk, tk), 0) <= lax.broadcasted_iota(jnp.int32, (tk, tk), 1)
    lam = (jnp.exp(jnp.sum(lam_ref[0:1, :] * lam_ref[1:2, :], axis=-1, keepdims=True))
           - jnp.exp(jnp.sum(lam_ref[2:3, :] * lam_ref[3:4, :], axis=-1, keepdims=True)) + lam_init)
    out_gain = sg_ref[...] * (1.0 - lam_init)

    def prepare_queries(a):
        qn = _group_rms(q_ref[0, a * tk:(a + 1) * tk, :].astype(F32), bd_ref[...], gq_ref[...])
        _store_split_qt(qn * (HEAD_DIM ** -0.5 * LOG2E), q1t_ref, q2t_ref, a * tk)

    def normed_keys(d):
        return _group_rms(k_ref[0, d * tk:(d + 1) * tk, :].astype(F32), bd_ref[...], gk_ref[...]).astype(BF16)

    def values_t(d):
        return _with_ones(v_ref[0, d * tk:(d + 1) * tk, :].astype(F32).T)

    def biased(z, a, d):
        if d == a:
            return jnp.where(causal, z + bias_ref[0, 0], NEG_BIG)
        return z + bias_ref[0, 1] if d == a - 1 else z

    def finalize(a):
        for c in range(a * tk // LANES, (a + 1) * tk // LANES):
            cols = slice(c * LANES, (c + 1) * LANES)
            o1 = acc1_ref[0:dv, cols] / acc1_ref[dv:dv + 1, cols]
            o2 = acc2_ref[0:dv, cols] / acc2_ref[dv:dv + 1, cols]
            o = o1 - lam * o2
            o = o * lax.rsqrt(jnp.mean(o * o, axis=0, keepdims=True) + RMS_EPS)
            o_ref[0, cols, :] = (o.T * out_gain).astype(o_ref.dtype)

    def bounded_sweep():
        def prepare(item, _):
            a, d = item
            if d == a:
                prepare_queries(a)
                kn_ref[a * tk:(a + 1) * tk, :] = normed_keys(a)
                vt_ref[a] = values_t(a)

        def logits(item, _):
            a, d = item
            kn = kn_ref[d * tk:(d + 1) * tk, :]
            return [biased(_dot(kn, qt_ref[:, a * tk:(a + 1) * tk]), a, d) for qt_ref in (q1t_ref, q2t_ref)]

        def probs(item, zs):
            return [jnp.exp2(z).astype(BF16) for z in zs]

        def values(item, ps):
            a, d = item
            cols = slice(a * tk, (a + 1) * tk)
            for acc_ref, p in zip((acc1_ref, acc2_ref), ps):
                if d == a:
                    acc_ref[:, cols] = _dot(vt_ref[d], p)
                else:
                    acc_ref[:, cols] += _dot(vt_ref[d], p)
            if d == 0:
                finalize(a)

        items = [(a, d) for a in range(nblk) for d in _descending(a + 1)]
        _pipeline(items, (prepare, logits, probs, values), lag=3)

    def running_max_sweep():
        m = [[None] * nblk for _ in range(2)]

        def queries(d, _):
            prepare_queries(d)

        def logits(d, _):
            kn = normed_keys(d)
            zs = []
            for qt_ref in (q1t_ref, q2t_ref):
                z = _dot(kn, qt_ref[:, d * tk:])
                zs.append(jnp.concatenate([biased(z[:, (a - d) * tk:(a - d + 1) * tk], a, d) for a in range(d, min(d + 2, nblk))]
                                          + ([z[:, 2 * tk:]] if d < nblk - 2 else []), axis=1))
            return zs

        def probs(d, zs):
            return _softmax_probs(d, zs, m, tk)

        def values(d, ps):
            vt = values_t(d)
            _accumulate(ps, (acc1_ref, acc2_ref), (vt, vt), tk)

        _pipeline(_descending(nblk), (queries, logits, probs, values))
        for a in range(nblk):
            finalize(a)

    is_bounded = bounded_ref[pl.program_id(1)] > 0
    pl.when(is_bounded)(bounded_sweep)
    pl.when(jnp.logical_not(is_bounded))(running_max_sweep)


def _diff_call(proj, bias, gq, gk, lam_rows, subln_g, n_heads, lam_init):
    b, s, _ = proj.shape
    tk = _key_tile(s)
    tok = lambda col: pl.BlockSpec((1, s, LANES), lambda i, p: (i, 0, col + p))
    const = lambda shape: pl.BlockSpec(shape, lambda i, p: (0,) * len(shape))
    dv = 2 * HEAD_DIM
    assert dv == LANES
    rows_aug = dv + ONES_ROWS
    return pl.pallas_call(
        functools.partial(_diff_kernel, lam_init=lam_init), grid=(b, n_heads),
        in_specs=[pl.BlockSpec(memory_space=pltpu.SMEM), tok(0), tok(n_heads), tok(2 * n_heads),
                  pl.BlockSpec((1, 2, tk, tk), lambda i, p: (p, 0, 0, 0)),
                  const((1, LANES)), const((1, LANES)), const((LANES, LANES)), const((8, LANES)), const((1, dv))],
        out_specs=tok(0),
        out_shape=jax.ShapeDtypeStruct((b, s, n_heads * dv), BF16),
        scratch_shapes=[pltpu.VMEM((LANES, s), BF16), pltpu.VMEM((LANES, s), BF16),
                        pltpu.VMEM((rows_aug, s), F32), pltpu.VMEM((rows_aug, s), F32),
                        pltpu.VMEM((s, LANES), BF16), pltpu.VMEM((s // tk, rows_aug, tk), BF16)],
        compiler_params=_params(2), name="diff_attn")(
            _bounded_flags(gq, gk, jnp.max(jnp.abs(bias), axis=(1, 2, 3))), proj, proj, proj, bias, gq, gk,
            _block_diag_ones(), lam_rows, subln_g)


def _t5_bucket(dist):
    max_exact = N_BUCKETS // 2
    nf = jnp.maximum(dist, 1).astype(F32)
    large = max_exact + (jnp.log(nf / max_exact) / math.log(MAX_DISTANCE / max_exact)
                         * (N_BUCKETS - max_exact)).astype(jnp.int32)
    large = jnp.minimum(large, N_BUCKETS - 1)
    return jnp.where(dist < max_exact, dist, large)


def _bias_tiles(rel_bias, t):
    assert t >= MAX_DISTANCE
    table = rel_bias.astype(F32)
    krow = lax.broadcasted_iota(jnp.int32, (2, t, t), 1)
    qcol = lax.broadcasted_iota(jnp.int32, (2, t, t), 2)
    dist = qcol - krow + t * lax.broadcasted_iota(jnp.int32, (2, t, t), 0)
    onehot = (_t5_bucket(jnp.maximum(dist, 0))[..., None] == jnp.arange(N_BUCKETS)).astype(F32)
    return jnp.einsum('otkb,bh->hotk', onehot, (table - table[N_BUCKETS - 1]) * LOG2E,
                      precision=lax.Precision.HIGHEST)


def _post_kernel(*refs, n_mix):
    x_ref = refs[0]
    o_refs = refs[1:1 + n_mix]
    wo_ref, g_ref, wg_ref, wu_ref, wd_ref, out_ref, x1_ref, act_ref = refs[1 + n_mix:]
    tm, d = x_ref.shape[1:]
    d_ff = wg_ref.shape[1]
    halves = _row_groups(tm)
    for rows in halves:
        o = jnp.concatenate([o_ref[0, rows, :] for o_ref in o_refs], axis=1) if n_mix > 1 else o_refs[0][0, rows, :]
        for lo in range(0, d, PROJ_COL_TILE):
            cols = slice(lo, lo + PROJ_COL_TILE)
            x1_ref[rows, cols] = x_ref[0, rows, cols] + _dot(o, wo_ref[:, cols])
    for rows in halves:
        h = _rms_rows(x1_ref[rows, :], g_ref[...]).astype(BF16)
        for lo in range(0, d_ff, FFN_COL_TILE):
            cols = slice(lo, min(lo + FFN_COL_TILE, d_ff))
            a = _dot(h, wg_ref[:, cols])
            u = _dot(h, wu_ref[:, cols])
            act_ref[rows, cols] = (a * (1.0 / (1.0 + jnp.exp(-a))) * u).astype(BF16)
    for rows in halves:
        for lo in range(0, d, PROJ_COL_TILE):
            cols = slice(lo, lo + PROJ_COL_TILE)
            out_ref[0, rows, cols] = x1_ref[rows, cols] + _dot(act_ref[rows, :], wd_ref[:, cols])


def _post_call(x, mixes, w_out, out_idx, g, wg, wu, wd, layer):
    b, s, d = x.shape
    tm = min(TOKEN_TILE, s)
    n_mix = len(mixes)
    tok = lambda width: pl.BlockSpec((1, tm, width), lambda i, j: (i, j, 0))
    res = lambda shape: pl.BlockSpec(shape, lambda i, j: (0, 0), pipeline_mode=pl.Buffered(1))
    layer_of = lambda w, idx: pl.BlockSpec((None,) + w.shape[1:], lambda i, j: (idx, 0, 0),
                                           pipeline_mode=pl.Buffered(1))
    return pl.pallas_call(
        functools.partial(_post_kernel, n_mix=n_mix), grid=(b, s // tm),
        in_specs=([tok(d)] + [tok(m.shape[2]) for m in mixes]
                  + [layer_of(w_out, out_idx), res((1, d)), layer_of(wg, layer), layer_of(wu, layer),
                     layer_of(wd, layer)]),
        out_specs=tok(d),
        out_shape=jax.ShapeDtypeStruct((b, s, d), x.dtype),
        scratch_shapes=[pltpu.VMEM((tm, d), F32), pltpu.VMEM((tm, wg.shape[2]), BF16)],
        compiler_params=_params(2), name="post")(x, *mixes, w_out, g, wg, wu, wd)


def _pair_gain(g):
    return jnp.concatenate([g, g]).astype(F32)[None, :]


def kernel(x, attn_norm_g, ffn_norm_g, even_w_in, fox_forget_b, fox_q_norm_g, fox_k_norm_g, even_w_out, diff_w_in, diff_q_norm_g, diff_k_norm_g, diff_lambda_q1, diff_lambda_k1, diff_lambda_q2, diff_lambda_k2, diff_subln_g, diff_w_out, rel_bias, ffn_w_gate, ffn_w_up, ffn_w_down):
    b, s, d = x.shape
    depth = attn_norm_g.shape[0]
    n_sb = d // (2 * HEAD_DIM)
    n_fox = d // (2 * HEAD_DIM)
    n_diff = d // (2 * HEAD_DIM)
    sb_w = n_sb * HEAD_DIM
    fox_w = n_fox * HEAD_DIM
    main_w = 3 * sb_w + 3 * fox_w
    assert sb_w % LANES == 0 and fox_w % LANES == 0 and n_fox <= LANES
    bias = _bias_tiles(rel_bias, _key_tile(s))
    even_w_in_bf, diff_w_in_bf = even_w_in.astype(BF16), diff_w_in.astype(BF16)
    even_w_out_bf, diff_w_out_bf = even_w_out.astype(BF16), diff_w_out.astype(BF16)
    wg_bf, wu_bf, wd_bf = ffn_w_gate.astype(BF16), ffn_w_up.astype(BF16), ffn_w_down.astype(BF16)

    for layer in range(depth):
        g_attn = attn_norm_g[layer][None, :]
        if layer % 2 == 0:
            e = layer // 2
            wfg = jnp.pad(even_w_in_bf[e][:, main_w:], ((0, 0), (0, LANES - n_fox)))
            fb = jnp.pad(fox_forget_b[e].astype(F32), (0, LANES - n_fox))[None, :]
            proj, c = _pre_call(x, g_attn, even_w_in_bf, e, main_w, (wfg, fb))
            sbc, fxc = sb_w // LANES, fox_w // LANES
            o_a = _sb_call(proj, sbc, 0, sbc, 2 * sbc)
            o_b = _fox_call(proj, c, _pair_gain(fox_q_norm_g[e]), _pair_gain(fox_k_norm_g[e]),
                            fxc, 3 * sbc, 3 * sbc + fxc, 3 * sbc + 2 * fxc)
            mixes, w_out, out_idx = [o_a, o_b], even_w_out_bf, e
        else:
            o = layer // 2
            proj = _pre_call(x, g_attn, diff_w_in_bf, o, diff_w_in_bf.shape[2])
            lam_rows = jnp.stack([diff_lambda_q1[o], diff_lambda_k1[o], diff_lambda_q2[o], diff_lambda_k2[o]])
            lam_rows = jnp.pad(lam_rows.astype(F32), ((0, 4), (0, LANES - HEAD_DIM)))
            lam_init = 0.8 - 0.6 * math.exp(-0.3 * layer)
            mix = _diff_call(proj, bias, _pair_gain(diff_q_norm_g[o]), _pair_gain(diff_k_norm_g[o]),
                             lam_rows, diff_subln_g[o].astype(F32)[None, :], n_diff, lam_init)
            mixes, w_out, out_idx = [mix], diff_w_out_bf, o
        x = _post_call(x, mixes, w_out, out_idx, ffn_norm_g[layer][None, :], wg_bf, wu_bf, wd_bf, layer)
    return x
```
